```python
import math
import jax, jax.numpy as jnp
from jax import lax
import numpy as np

D_MODEL = 1024
BATCH = 16
SEQ = 2048
DEPTH = 2

CHUNK = 64
Q_BLOCK = 128

MLA_HEADS = 8
QK_NOPE_DIM = 64
QK_ROPE_DIM = 32
QK_HEAD_DIM = QK_NOPE_DIM + QK_ROPE_DIM
V_HEAD_DIM = 64
MLA_WIDTH = MLA_HEADS * V_HEAD_DIM
Q_LORA_RANK = 384
KV_LORA_RANK = 256
ROPE_THETA = 10000.0

SSM_WIDTH = 512
SSM_GROUP = 16
SSM_GROUPS = SSM_WIDTH // SSM_GROUP
SSM_STATE = 64
DT_MIN = 0.001
DT_MAX = 0.1

N_BRANCH = 2

COL_SPLITS = [
    Q_LORA_RANK,
    Q_LORA_RANK + KV_LORA_RANK,
    Q_LORA_RANK + KV_LORA_RANK + QK_ROPE_DIM,
    Q_LORA_RANK + KV_LORA_RANK + QK_ROPE_DIM + SSM_WIDTH,
]
IN_COLS = COL_SPLITS[-1] + N_BRANCH * D_MODEL

N_EXPERT_GROUPS = 4
EXPERTS_PER_GROUP = 8
N_EXPERTS = N_EXPERT_GROUPS * EXPERTS_PER_GROUP
TOP_K = 2
D_EXPERT = 256

EPS = 1e-6

kernel_name = "hybrid_mla_s5_hmoe_streaming"


def rmsnorm(x, g):
    xf = x.astype(jnp.float32)
    y = xf * lax.rsqrt(jnp.mean(xf * xf, axis=-1, keepdims=True) + EPS)
    return (y * g.astype(jnp.float32)).astype(x.dtype)


def rope_tables(positions):
    half = QK_ROPE_DIM // 2
    inv_freq = ROPE_THETA ** (-jnp.arange(half, dtype=jnp.float32) / half)
    ang = positions.astype(jnp.float32)[..., None] * inv_freq
    return jnp.cos(ang)[:, :, None, :], jnp.sin(ang)[:, :, None, :]


def apply_rope(x, cos, sin):
    half = QK_ROPE_DIM // 2
    x1, x2 = x[..., :half], x[..., half:]
    return jnp.concatenate([x1 * cos - x2 * sin, x2 * cos + x1 * sin], axis=-1).astype(x.dtype)


def mla_branch(q_c, kv_c, k_r, cos, sin, q_lora_g, w_uq, kv_lora_g, w_ukv,
               q_head_g, k_head_g, w_attn_proj):
    bsz, seq, _ = q_c.shape
    q = (rmsnorm(q_c, q_lora_g) @ w_uq).reshape(bsz, seq, MLA_HEADS, QK_HEAD_DIM)
    kv = (rmsnorm(kv_c, kv_lora_g) @ w_ukv).reshape(bsz, seq, MLA_HEADS, QK_NOPE_DIM + V_HEAD_DIM)
    k_nope, v = kv[..., :QK_NOPE_DIM], kv[..., QK_NOPE_DIM:]
    k_pe = jnp.broadcast_to(k_r[:, :, None, :], (bsz, seq, MLA_HEADS, QK_ROPE_DIM))
    k = jnp.concatenate([k_nope, k_pe], axis=-1)
    q = rmsnorm(q, q_head_g)
    k = rmsnorm(k, k_head_g)
    q = jnp.concatenate([q[..., :QK_NOPE_DIM], apply_rope(q[..., QK_NOPE_DIM:], cos, sin)], axis=-1)
    k = jnp.concatenate([k[..., :QK_NOPE_DIM], apply_rope(k[..., QK_NOPE_DIM:], cos, sin)], axis=-1)

    scale = QK_HEAD_DIM ** -0.5
    outs = []
    for qs in range(0, seq, Q_BLOCK):
        ke = qs + Q_BLOCK
        s = jnp.einsum('bqhd,bkhd->bhqk', q[:, qs:ke], k[:, :ke],
                       preferred_element_type=jnp.float32) * scale
        q_chunk = (qs + jnp.arange(Q_BLOCK)) // CHUNK
        k_chunk = jnp.arange(ke) // CHUNK
        mask = k_chunk[None, :] <= q_chunk[:, None]
        s = jnp.where(mask, s, jnp.finfo(jnp.float32).min)
        p = jax.nn.softmax(s, axis=-1).astype(v.dtype)
        outs.append(jnp.einsum('bhqk,bkhd->bqhd', p, v[:, :ke]))
    o = jnp.concatenate(outs, axis=1).reshape(bsz, seq, MLA_WIDTH)
    return o @ w_attn_proj


def _scan_combine(e1, e2):
    a1, b1 = e1
    a2, b2 = e2
    return a1 * a2, a2 * b1 + b2


def s5_branch(u, lam_re, lam_im, log_dt, b_re, b_im, c_re, c_im, d_skip, w_glu):
    bsz, seq, _ = u.shape
    uf = u.astype(jnp.float32).reshape(bsz, seq, SSM_GROUPS, SSM_GROUP)
    lam = lax.complex(lam_re.astype(jnp.float32), lam_im.astype(jnp.float32))
    dt = jnp.exp(log_dt.astype(jnp.float32))[:, None]
    lam_bar = jnp.exp(lam * dt)
    b = lax.complex(b_re.astype(jnp.float32), b_im.astype(jnp.float32))
    b_bar = ((lam_bar - 1.0) / lam)[..., None] * b
    bu = jnp.einsum('blgi,gpi->blgp', uf.astype(jnp.complex64), b_bar)
    a = jnp.broadcast_to(lam_bar[None, None], (1, seq, SSM_GROUPS, SSM_STATE))
    _, states = lax.associative_scan(_scan_combine, (a, bu), axis=1)
    c = lax.complex(c_re.astype(jnp.float32), c_im.astype(jnp.float32))
    y = jnp.einsum('blgp,gip->blgi', states, c).real + d_skip.astype(jnp.float32) * uf
    y = jax.nn.gelu(y.reshape(bsz, seq, SSM_WIDTH)).astype(u.dtype)
    vg = y @ w_glu
    val, gate = vg[..., :D_MODEL], vg[..., D_MODEL:]
    return val * jax.nn.sigmoid(gate)


def hier_moe(h, w_router_group, w_router_expert, w_exp_gate, w_exp_up, w_exp_down):
    bsz, seq, d = h.shape
    t = h.reshape(-1, d)
    n_tok = t.shape[0]
    g_prob = jax.nn.softmax((t @ w_router_group).astype(jnp.float32), axis=-1)
    g_idx = jnp.argmax(g_prob, axis=-1)
    g_w = jnp.take_along_axis(g_prob, g_idx[:, None], axis=-1)
    e_logits = (t @ w_router_expert).astype(jnp.float32).reshape(n_tok, N_EXPERT_GROUPS, EXPERTS_PER_GROUP)
    e_logits = jnp.take_along_axis(e_logits, g_idx[:, None, None], axis=1)[:, 0]
    e_prob = jax.nn.softmax(e_logits, axis=-1)
    top_w, top_i = lax.top_k(e_prob, TOP_K)
    top_w = top_w / jnp.sum(top_w, axis=-1, keepdims=True) * g_w
    eid = g_idx[:, None] * EXPERTS_PER_GROUP + top_i
    combine = jnp.sum(jax.nn.one_hot(eid, N_EXPERTS, dtype=jnp.float32) * top_w[..., None], axis=1)
    out = jnp.zeros((n_tok, d), jnp.float32)
    for e in range(N_EXPERTS):
        hidden = jax.nn.silu(t @ w_exp_gate[e]) * (t @ w_exp_up[e])
        out = out + (hidden @ w_exp_down[e]).astype(jnp.float32) * combine[:, e:e + 1]
    return out.astype(h.dtype).reshape(bsz, seq, d)


def setup_inputs(seed: int = 0) -> dict:
    key = jax.random.key(seed)
    ks = jax.random.split(key, 32)
    f32 = jnp.float32

    def nrm(k, shape, fan_in):
        return jax.random.normal(k, shape, f32) * (fan_in ** -0.5)

    def gain(k, shape):
        return 1.0 + 0.02 * jax.random.normal(k, shape, f32)

    x = jax.random.normal(ks[0], (BATCH, SEQ, D_MODEL), f32)
    offset = jax.random.randint(ks[1], (BATCH, 1), 0, 64, jnp.int32) * CHUNK
    positions = (offset + jnp.arange(SEQ, dtype=jnp.int32)[None, :]).astype(jnp.int32)

    n_idx = jnp.arange(SSM_STATE, dtype=f32)
    lam_re = -0.5 * (1.0 + 0.02 * jax.random.normal(ks[2], (DEPTH, SSM_GROUPS, SSM_STATE), f32))
    lam_im = math.pi * n_idx[None, None, :] + 0.01 * jax.random.normal(ks[3], (DEPTH, SSM_GROUPS, SSM_STATE), f32)
    log_dt = jax.random.uniform(ks[4], (DEPTH, SSM_GROUPS), f32, math.log(DT_MIN), math.log(DT_MAX))

    return {
        "x": x,
        "positions": positions,
        "norm_mix_g": gain(ks[5], (DEPTH, D_MODEL)),
        "w_in": nrm(ks[6], (DEPTH, D_MODEL, IN_COLS), D_MODEL),
        "q_lora_g": gain(ks[7], (DEPTH, Q_LORA_RANK)),
        "w_uq": nrm(ks[8], (DEPTH, Q_LORA_RANK, MLA_HEADS * QK_HEAD_DIM), Q_LORA_RANK),
        "kv_lora_g": gain(ks[9], (DEPTH, KV_LORA_RANK)),
        "w_ukv": nrm(ks[10], (DEPTH, KV_LORA_RANK, MLA_HEADS * (QK_NOPE_DIM + V_HEAD_DIM)), KV_LORA_RANK),
        "q_head_g": gain(ks[11], (DEPTH, QK_HEAD_DIM)),
        "k_head_g": gain(ks[12], (DEPTH, QK_HEAD_DIM)),
        "w_attn_proj": nrm(ks[13], (DEPTH, MLA_WIDTH, D_MODEL), MLA_WIDTH),
        "lam_re": lam_re,
        "lam_im": lam_im,
        "log_dt": log_dt,
        "b_re": nrm(ks[14], (DEPTH, SSM_GROUPS, SSM_STATE, SSM_GROUP), 2 * SSM_GROUP),
        "b_im": nrm(ks[15], (DEPTH, SSM_GROUPS, SSM_STATE, SSM_GROUP), 2 * SSM_GROUP),
        "c_re": nrm(ks[16], (DEPTH, SSM_GROUPS, SSM_GROUP, SSM_STATE), 2 * SSM_STATE),
        "c_im": nrm(ks[17], (DEPTH, SSM_GROUPS, SSM_GROUP, SSM_STATE), 2 * SSM_STATE),
        "d_skip": jax.random.normal(ks[18], (DEPTH, SSM_GROUPS, SSM_GROUP), f32),
        "w_glu": nrm(ks[19], (DEPTH, SSM_WIDTH, 2 * D_MODEL), SSM_WIDTH),
        "w_out": nrm(ks[20], (DEPTH, D_MODEL, D_MODEL), D_MODEL),
        "norm_ffn_g": gain(ks[21], (DEPTH, D_MODEL)),
        "w_router_group": nrm(ks[22], (DEPTH, D_MODEL, N_EXPERT_GROUPS), D_MODEL),
        "w_router_expert": nrm(ks[23], (DEPTH, D_MODEL, N_EXPERTS), D_MODEL),
        "w_exp_gate": nrm(ks[24], (DEPTH, N_EXPERTS, D_MODEL, D_EXPERT), D_MODEL),
        "w_exp_up": nrm(ks[25], (DEPTH, N_EXPERTS, D_MODEL, D_EXPERT), D_MODEL),
        "w_exp_down": nrm(ks[26], (DEPTH, N_EXPERTS, D_EXPERT, D_MODEL), D_EXPERT),
    }


def reference(x, positions, norm_mix_g, w_in, q_lora_g, w_uq, kv_lora_g, w_ukv,
              q_head_g, k_head_g, w_attn_proj, lam_re, lam_im, log_dt, b_re, b_im,
              c_re, c_im, d_skip, w_glu, w_out, norm_ffn_g, w_router_group,
              w_router_expert, w_exp_gate, w_exp_up, w_exp_down):
    bsz, seq, _ = x.shape
    cos, sin = rope_tables(positions)
    for l in range(DEPTH):
        h = rmsnorm(x, norm_mix_g[l])
        proj = h @ w_in[l]
        q_c, kv_c, k_r, u, gates = jnp.split(proj, COL_SPLITS, axis=-1)
        y_attn = mla_branch(q_c, kv_c, k_r, cos, sin, q_lora_g[l], w_uq[l], kv_lora_g[l],
                            w_ukv[l], q_head_g[l], k_head_g[l], w_attn_proj[l])
        y_ssm = s5_branch(u, lam_re[l], lam_im[l], log_dt[l], b_re[l], b_im[l],
                          c_re[l], c_im[l], d_skip[l], w_glu[l])
        g = jax.nn.sigmoid(gates).reshape(bsz, seq, N_BRANCH, D_MODEL)
        merged = g[:, :, 0, :] * y_attn + g[:, :, 1, :] * y_ssm
        x = x + merged @ w_out[l]
        h = rmsnorm(x, norm_ffn_g[l])
        x = x + hier_moe(h, w_router_group[l], w_router_expert[l], w_exp_gate[l],
                         w_exp_up[l], w_exp_down[l])
    return x
```

```python
import functools
import math

import jax
import jax.numpy as jnp
from jax import lax
from jax.experimental import pallas as pl
from jax.experimental.pallas import tpu as pltpu

F32 = jnp.float32
BF16 = jnp.bfloat16

D_MODEL = 1024
CHUNK = 64
HEADS = 8
NOPE = 64
ROPE = 32
QK_DIM = NOPE + ROPE
V_DIM = 64
Q_LORA = 384
KV_LORA = 256
ROPE_THETA = 10000.0
SSM_WIDTH = 512
SSM_GROUP = 16
SSM_GROUPS = 32
SSM_STATE = 64
N_GROUPS = 4
PER_GROUP = 8
N_EXPERTS = 32
D_EXPERT = 256
EPS = 1e-6

LANES = 128
SSM_CHUNK = 16
SSM_QBLOCKS = SSM_WIDTH // LANES
GROUPS_PER_QBLOCK = LANES // SSM_GROUP
STATE_COLS = 2 * GROUPS_PER_QBLOCK * SSM_STATE
ROW_TILE = 256
VMEM_LIMIT = 56 * 1024 * 1024

C_Q = 0
C_KV = C_Q + Q_LORA
C_KR = C_KV + KV_LORA
C_U = C_KR + LANES
C_GATE = C_U + SSM_WIDTH
IN_COLS_PAD = C_GATE + 2 * D_MODEL


def _rms(x, g):
    return x * lax.rsqrt(jnp.mean(x * x, axis=-1, keepdims=True) + EPS) * g


def _dot(a, b):
    return jnp.dot(a, b, preferred_element_type=F32)


def _k1_body(x_ref, gmix_ref, win_ref, gql_ref, wuq_ref, gkvl_ref, wukv_ref, gq_ref, gk_ref,
             c_ref, sm_ref, sp_ref, q_ref, k_ref, v_ref, u_ref, gates_ref):
    hb = _rms(x_ref[...], gmix_ref[...]).astype(BF16)

    def proj(lo, hi):
        return _dot(hb, win_ref[:, lo:hi])

    cos_t = c_ref[...]
    sin_m = sm_ref[...]
    sin_p = sp_ref[...]

    def head_norm_rope(r, g):
        ms = jnp.sum(r * r, axis=-1, keepdims=True) * (1.0 / QK_DIM)
        rn = r * lax.rsqrt(ms + EPS) * g
        return (rn * cos_t + pltpu.roll(rn, LANES - ROPE // 2, 1) * sin_m
                + pltpu.roll(rn, ROPE // 2, 1) * sin_p)

    qn = _rms(proj(C_Q, C_KV), gql_ref[...]).astype(BF16)
    qf = _dot(qn, wuq_ref[...])
    for h in range(HEADS):
        sl = slice(h * LANES, (h + 1) * LANES)
        q_ref[:, sl] = head_norm_rope(qf[:, sl], gq_ref[:, sl]).astype(BF16)

    kvn = _rms(proj(C_KV, C_KR), gkvl_ref[...]).astype(BF16)
    kf = _dot(kvn, wukv_ref[...])
    kr = proj(C_KR, C_U)
    for h in range(HEADS):
        sl = slice(h * LANES, (h + 1) * LANES)
        k_ref[:, sl] = head_norm_rope(kf[:, sl] + kr, gk_ref[:, sl]).astype(BF16)
    v_ref[...] = kf[:, HEADS * LANES:].astype(BF16)

    u_ref[...] = proj(C_U, C_GATE)
    gates_ref[...] = jax.nn.sigmoid(proj(C_GATE, IN_COLS_PAD)).astype(BF16)


def _k1(x, gmix, win, gql, wuq, gkvl, wukv, gq, gk, cos_t, sin_m, sin_p, tm):
    t = x.shape[0]
    row = lambda w: pl.BlockSpec((tm, w), lambda i: (i, 0))
    full = lambda a: pl.BlockSpec(a.shape, lambda i: (0,) * a.ndim)
    return pl.pallas_call(
        _k1_body,
        grid=(t // tm,),
        in_specs=[row(D_MODEL), full(gmix), full(win), full(gql), full(wuq), full(gkvl), full(wukv),
                  full(gq), full(gk), row(LANES), row(LANES), row(LANES)],
        out_specs=[row(HEADS * LANES), row(HEADS * LANES), row(HEADS * V_DIM), row(SSM_WIDTH),
                   row(2 * D_MODEL)],
        out_shape=[jax.ShapeDtypeStruct((t, HEADS * LANES), BF16),
                   jax.ShapeDtypeStruct((t, HEADS * LANES), BF16),
                   jax.ShapeDtypeStruct((t, HEADS * V_DIM), BF16),
                   jax.ShapeDtypeStruct((t, SSM_WIDTH), F32),
                   jax.ShapeDtypeStruct((t, 2 * D_MODEL), BF16)],
        compiler_params=pltpu.CompilerParams(dimension_semantics=("arbitrary",),
                                             vmem_limit_bytes=VMEM_LIMIT),
        name="k1_inproj",
    )(x, gmix, win, gql, wuq, gkvl, wukv, gq, gk, cos_t, sin_m, sin_p)


def _attn_body(q_ref, k_ref, v_ref, o_ref, *, tq):
    i = pl.program_id(2)
    row_chunk = lax.broadcasted_iota(jnp.int32, (tq, tq), 0) // CHUNK
    col_chunk = lax.broadcasted_iota(jnp.int32, (tq, tq), 1) // CHUNK
    diag_mask = col_chunk <= row_chunk
    neg = jnp.float32(-1e30)
    outs = []
    for hh in range(2):
        sl = slice(hh * LANES, (hh + 1) * LANES)
        q = q_ref[0, :, sl]

        def block(j, carry, masked):
            m, l, acc = carry
            start = pl.multiple_of(j * tq, tq)
            kb = k_ref[0, pl.ds(start, tq), sl]
            vb = v_ref[0, pl.ds(start, tq), :]
            s = lax.dot_general(q, kb, (((1,), (1,)), ((), ())), preferred_element_type=F32)
            if masked:
                s = jnp.where(diag_mask, s, neg)
            m_new = jnp.maximum(m, jnp.max(s, axis=-1, keepdims=True))
            p = jnp.exp(s - m_new)
            alpha = jnp.exp(m - m_new)
            l_new = alpha * l + jnp.sum(p, axis=-1, keepdims=True)
            acc_new = alpha * acc + _dot(p.astype(BF16), vb)
            return m_new, l_new, acc_new

        init = (jnp.full((tq, 1), neg, F32), jnp.zeros((tq, 1), F32), jnp.zeros((tq, LANES), F32))
        carry = lax.fori_loop(0, i, lambda j, c: block(j, c, False), init)
        m, l, acc = block(i, carry, True)
        outs.append(acc / l)
    lane = lax.broadcasted_iota(jnp.int32, (tq, LANES), 1)
    o_ref[0] = jnp.where(lane < V_DIM, outs[0], outs[1]).astype(BF16)


def _attention(q, k, v, tq):
    b, l, _ = q.shape
    return pl.pallas_call(
        functools.partial(_attn_body, tq=tq),
        grid=(b, HEADS // 2, l // tq),
        in_specs=[pl.BlockSpec((1, tq, 2 * LANES), lambda bi, hp, i: (bi, i, hp)),
                  pl.BlockSpec((1, l, 2 * LANES), lambda bi, hp, i: (bi, 0, hp)),
                  pl.BlockSpec((1, l, 2 * V_DIM), lambda bi, hp, i: (bi, 0, hp))],
        out_specs=pl.BlockSpec((1, tq, 2 * V_DIM), lambda bi, hp, i: (bi, i, hp)),
        out_shape=jax.ShapeDtypeStruct((b, l, HEADS * V_DIM), BF16),
        compiler_params=pltpu.CompilerParams(
            dimension_semantics=("arbitrary", "arbitrary", "arbitrary"), vmem_limit_bytes=VMEM_LIMIT),
        name="attention",
    )(q, k, v)


def _ssm_weights(lam_re, lam_im, log_dt, b_re, b_im, c_re, c_im, d_skip):
    lc = SSM_CHUNK
    nq, gq = SSM_QBLOCKS, GROUPS_PER_QBLOCK
    lam = lax.complex(lam_re.astype(F32), lam_im.astype(F32))
    dt = jnp.exp(log_dt.astype(F32))[:, None]
    lam_bar = jnp.exp(lam * dt)
    b_bar = ((lam_bar - 1.0) / lam)[..., None] * lax.complex(b_re.astype(F32), b_im.astype(F32))
    c = lax.complex(c_re.astype(F32), c_im.astype(F32))
    steps = jnp.arange(lc + 1, dtype=F32)[:, None, None]
    pw = jnp.exp((lam * dt)[None] * steps)
    eye = jnp.eye(gq, dtype=F32)

    kern = jnp.einsum('gip,dgp,gpj->dgij', c, pw[:lc], b_bar).real
    kern = kern.at[0].add(jnp.eye(SSM_GROUP, dtype=F32)[None] * d_skip.astype(F32)[:, :, None])
    kpad = jnp.concatenate([jnp.zeros_like(kern[:1]), kern], axis=0)
    big_d = jnp.arange(lc // 2)
    rows = []
    for s2 in range(2):
        cols = []
        for t2 in range(2):
            kd = kpad[2 * big_d + t2 - s2 + 1]
            kd = kd.reshape(lc // 2, nq, gq, SSM_GROUP, SSM_GROUP)
            cols.append(jnp.einsum('Dqgij,gh->qDgjhi', kd, eye))
        rows.append(jnp.stack(cols, axis=4))
    w_toep = jnp.stack(rows, axis=2).reshape(nq, lc // 2, 2 * LANES, 2 * LANES)

    vin = pw[:lc][::-1][:, :, :, None] * b_bar[None]
    vin = vin.reshape(lc // 2, 2, nq, gq, SSM_STATE, SSM_GROUP)
    w_in = jnp.stack([jnp.einsum('Ssqgpj,gh->qSsgjhp', part, eye) for part in (vin.real, vin.imag)],
                     axis=5)
    w_in = w_in.reshape(nq, lc // 2, 2 * LANES, STATE_COLS)

    cw = c[None] * pw[1:lc + 1][:, :, None, :]
    cw = cw.reshape(lc // 2, 2, nq, gq, SSM_GROUP, SSM_STATE)
    w_out = jnp.stack([jnp.einsum('Ttqgip,gh->qTgpthi', part, eye) for part in (cw.real, -cw.imag)],
                      axis=2)
    w_out = w_out.reshape(nq, lc // 2, STATE_COLS, 2 * LANES)

    a_chunk = pw[lc].reshape(nq, gq * SSM_STATE)
    a_chunk = jnp.stack([a_chunk.real, a_chunk.imag], axis=1)
    return w_toep.astype(BF16), w_in.astype(BF16), w_out.astype(BF16), a_chunk


def _ssm_body(u_ref, wt_ref, win_ref, wout_ref, a_ref, y_ref, st_ref, hp_ref, *, bb, nc):
    half = STATE_COLS // 2
    n_pair = SSM_CHUNK // 2
    xs = []
    for s in range(n_pair):
        per_batch = []
        for b in range(bb):
            x0 = u_ref[b, pl.ds(2 * s, nc, stride=SSM_CHUNK), :]
            x1 = u_ref[b, pl.ds(2 * s + 1, nc, stride=SSM_CHUNK), :]
            per_batch.append(jnp.concatenate([x0, x1], axis=1))
        xs.append(jnp.concatenate(per_batch, axis=0).astype(BF16))

    st = _dot(xs[0], win_ref[0, 0])
    for s in range(1, n_pair):
        st = st + _dot(xs[s], win_ref[0, s])
    st_ref[...] = st

    ar = a_ref[0, 0:1, :]
    ai = a_ref[0, 1:2, :]

    def step(c, carry):
        new = []
        for b in range(bb):
            hr, hi = carry[b]
            row = b * nc + c
            hp_ref[pl.ds(row, 1), :] = jnp.concatenate([hr, hi], axis=1)
            srow = st_ref[pl.ds(row, 1), :]
            new.append((ar * hr - ai * hi + srow[:, :half], ar * hi + ai * hr + srow[:, half:]))
        return tuple(new)

    zero = jnp.zeros((1, half), F32)
    lax.fori_loop(0, nc, step, tuple((zero, zero) for _ in range(bb)))

    hp = hp_ref[...].astype(BF16)
    for t in range(n_pair):
        acc = _dot(hp, wout_ref[0, t])
        for s in range(t + 1):
            acc = acc + _dot(xs[s], wt_ref[0, t - s])
        yg = jax.nn.gelu(acc)
        for b in range(bb):
            rows = slice(b * nc, (b + 1) * nc)
            y_ref[b, pl.ds(2 * t, nc, stride=SSM_CHUNK), :] = yg[rows, :LANES]
            y_ref[b, pl.ds(2 * t + 1, nc, stride=SSM_CHUNK), :] = yg[rows, LANES:]


def _ssm(u, w_toep, w_in, w_out, a_chunk, bb):
    b, l, _ = u.shape
    nc = l // SSM_CHUNK
    wspec = lambda a: pl.BlockSpec((1,) + a.shape[1:], lambda q, bi: (q,) + (0,) * (a.ndim - 1))
    return pl.pallas_call(
        functools.partial(_ssm_body, bb=bb, nc=nc),
        grid=(SSM_QBLOCKS, b // bb),
        in_specs=[pl.BlockSpec((bb, l, LANES), lambda q, bi: (bi, 0, q)),
                  wspec(w_toep), wspec(w_in), wspec(w_out), wspec(a_chunk)],
        out_specs=pl.BlockSpec((bb, l, LANES), lambda q, bi: (bi, 0, q)),
        out_shape=jax.ShapeDtypeStruct((b, l, SSM_WIDTH), F32),
        scratch_shapes=[pltpu.VMEM((bb * nc, STATE_COLS), F32), pltpu.VMEM((bb * nc, STATE_COLS), F32)],
        compiler_params=pltpu.CompilerParams(dimension_semantics=("arbitrary", "arbitrary"),
                                             vmem_limit_bytes=VMEM_LIMIT),
        name="ssm",
    )(u, w_toep, w_in, w_out, a_chunk)


def _k4_body(x_ref, o_ref, ys_ref, g_ref, wap_ref, wglu_ref, wout_ref, gffn_ref, wrh_ref, wrl_ref,
             xn_ref, h2_ref, route_ref, cnt_ref, carry_ref, *, tm):
    @pl.when(pl.program_id(0) == 0)
    def _():
        carry_ref[...] = jnp.zeros_like(carry_ref)

    y_attn = _dot(o_ref[...], wap_ref[...])
    vg = _dot(ys_ref[...].astype(BF16), wglu_ref[...])
    y_ssm = vg[:, :D_MODEL] * jax.nn.sigmoid(vg[:, D_MODEL:])
    g = g_ref[...].astype(F32)
    merged = g[:, :D_MODEL] * y_attn + g[:, D_MODEL:] * y_ssm
    xn = x_ref[...] + _dot(merged.astype(BF16), wout_ref[...])
    xn_ref[...] = xn
    h2 = _rms(xn, gffn_ref[...])
    h2_ref[...] = h2

    hh = h2.astype(BF16)
    hl = (h2 - hh.astype(F32)).astype(BF16)
    logits = _dot(hh, wrh_ref[...]) + _dot(hh, wrl_ref[...]) + _dot(hl, wrh_ref[...])

    lane = lax.broadcasted_iota(jnp.int32, (tm, LANES), 1).astype(F32)
    ninf = jnp.float32(-jnp.inf)
    big = jnp.float32(4 * LANES)

    def top(mask):
        val = jnp.max(jnp.where(mask, logits, ninf), axis=-1, keepdims=True)
        idx = jnp.min(jnp.where(mask & (logits == val), lane, big), axis=-1, keepdims=True)
        return val, idx

    gmask = lane < N_GROUPS
    gmax, gidx = top(gmask)
    g_w = 1.0 / jnp.sum(jnp.where(gmask, jnp.exp(logits - gmax), 0.0), axis=-1, keepdims=True)
    lo = N_GROUPS + PER_GROUP * gidx
    emask = (lane >= lo) & (lane < lo + PER_GROUP)
    l1, i1 = top(emask)
    l2, i2 = top(emask & (lane != i1))
    e21 = jnp.exp(l2 - l1)
    w1 = g_w / (1.0 + e21)
    w2 = g_w * e21 / (1.0 + e21)

    hot1 = lane == i1
    hot2 = lane == i2
    onehot = (hot1 | hot2).astype(BF16)
    tri = (lax.broadcasted_iota(jnp.int32, (tm, tm), 1)
           < lax.broadcasted_iota(jnp.int32, (tm, tm), 0)).astype(BF16)
    before = _dot(tri, onehot) + carry_ref[...]
    r1 = jnp.sum(jnp.where(hot1, before, 0.0), axis=-1, keepdims=True)
    r2 = jnp.sum(jnp.where(hot2, before, 0.0), axis=-1, keepdims=True)
    carry_ref[...] += jnp.sum(onehot.astype(F32), axis=0, keepdims=True)
    cnt_ref[...] = carry_ref[...]

    cols = (w1, w2, i1 - N_GROUPS, i2 - N_GROUPS, r1, r2)
    out = jnp.zeros((tm, LANES), F32)
    for n, col in enumerate(cols):
        out = jnp.where(lane == n, col, out)
    route_ref[...] = out


def _k4(x, o, ys, gates, wap, wglu, wout, gffn, wrh, wrl, tm):
    t = x.shape[0]
    row = lambda w: pl.BlockSpec((tm, w), lambda i: (i, 0))
    full = lambda a: pl.BlockSpec(a.shape, lambda i: (0,) * a.ndim)
    return pl.pallas_call(
        functools.partial(_k4_body, tm=tm),
        grid=(t // tm,),
        in_specs=[row(D_MODEL), row(HEADS * V_DIM), row(SSM_WIDTH), row(2 * D_MODEL),
                  full(wap), full(wglu), full(wout), full(gffn), full(wrh), full(wrl)],
        out_specs=[row(D_MODEL), row(D_MODEL), row(LANES), pl.BlockSpec((1, LANES), lambda i: (0, 0))],
        out_shape=[jax.ShapeDtypeStruct((t, D_MODEL), F32), jax.ShapeDtypeStruct((t, D_MODEL), F32),
                   jax.ShapeDtypeStruct((t, LANES), F32), jax.ShapeDtypeStruct((1, LANES), F32)],
        scratch_shapes=[pltpu.VMEM((1, LANES), F32)],
        compiler_params=pltpu.CompilerParams(dimension_semantics=("arbitrary",),
                                             vmem_limit_bytes=VMEM_LIMIT),
        name="k4_merge_router",
    )(x, o, ys, gates, wap, wglu, wout, gffn, wrh, wrl)


def _row_copy(src, dst, sem):
    return pltpu.make_async_copy(src, dst, sem)


def _dispatch_body(zt_ref, nz_ref, pos_ref, h2_ref, xs_ref, zeros_ref, sem, *, td):
    @pl.when(pl.program_id(0) == 0)
    def _():
        zeros_ref[...] = jnp.zeros_like(zeros_ref)
        n = nz_ref[0]

        def zstart(z, _):
            _row_copy(zeros_ref, xs_ref.at[pl.ds(pl.multiple_of(zt_ref[z] * ROW_TILE, ROW_TILE), ROW_TILE)],
                      sem).start()
            return 0

        def zwait(z, _):
            _row_copy(zeros_ref, xs_ref.at[pl.ds(0, ROW_TILE)], sem).wait()
            return 0

        lax.fori_loop(0, n, zstart, 0)
        lax.fori_loop(0, n, zwait, 0)

    def start(r, _):
        for kk in range(2):
            p = pos_ref[0, 0, 2 * r + kk]
            _row_copy(h2_ref.at[pl.ds(r, 1)], xs_ref.at[pl.ds(p, 1)], sem).start()
        return 0

    def wait(r, _):
        for kk in range(2):
            _row_copy(h2_ref.at[pl.ds(0, 1)], xs_ref.at[pl.ds(0, 1)], sem).wait()
        return 0

    lax.fori_loop(0, td, start, 0)
    lax.fori_loop(0, td, wait, 0)


def _dispatch(zero_tiles, n_zero, pos, h2, n_rows, td):
    t = h2.shape[0]
    grid_spec = pltpu.PrefetchScalarGridSpec(
        num_scalar_prefetch=2,
        grid=(t // td,),
        in_specs=[pl.BlockSpec((1, 1, 2 * td), lambda i, zt, nz: (i, 0, 0), memory_space=pltpu.SMEM),
                  pl.BlockSpec((td, D_MODEL), lambda i, zt, nz: (i, 0))],
        out_specs=pl.BlockSpec(memory_space=pl.ANY),
        scratch_shapes=[pltpu.VMEM((ROW_TILE, D_MODEL), F32), pltpu.SemaphoreType.DMA(())],
    )
    return pl.pallas_call(
        functools.partial(_dispatch_body, td=td),
        grid_spec=grid_spec,
        out_shape=jax.ShapeDtypeStruct((n_rows, D_MODEL), F32),
        compiler_params=pltpu.CompilerParams(dimension_semantics=("arbitrary",)),
        name="moe_dispatch",
    )(zero_tiles, n_zero, pos.reshape(t // td, 1, 2 * td), h2)


def _expert_body(te_ref, tv_ref, xi_ref, xs_ref, wg_ref, wu_ref, wd_ref, ys_ref):
    i = pl.program_id(0)

    @pl.when(tv_ref[i] == 1)
    def _():
        xb = xs_ref[...].astype(BF16)
        hidden = jax.nn.silu(_dot(xb, wg_ref[0])) * _dot(xb, wu_ref[0])
        ys_ref[...] = _dot(hidden.astype(BF16), wd_ref[0])

    @pl.when(tv_ref[i] == 0)
    def _():
        ys_ref[...] = jnp.zeros_like(ys_ref)


def _experts(tile_expert, tile_valid, tile_src, xs, wg, wu, wd):
    n_rows = xs.shape[0]
    grid_spec = pltpu.PrefetchScalarGridSpec(
        num_scalar_prefetch=3,
        grid=(n_rows // ROW_TILE,),
        in_specs=[pl.BlockSpec((ROW_TILE, D_MODEL), lambda i, te, tv, xi: (xi[i], 0)),
                  pl.BlockSpec((1, D_MODEL, D_EXPERT), lambda i, te, tv, xi: (te[i], 0, 0)),
                  pl.BlockSpec((1, D_MODEL, D_EXPERT), lambda i, te, tv, xi: (te[i], 0, 0)),
                  pl.BlockSpec((1, D_EXPERT, D_MODEL), lambda i, te, tv, xi: (te[i], 0, 0))],
        out_specs=pl.BlockSpec((ROW_TILE, D_MODEL), lambda i, te, tv, xi: (i, 0)),
    )
    return pl.pallas_call(
        _expert_body,
        grid_spec=grid_spec,
        out_shape=jax.ShapeDtypeStruct((n_rows, D_MODEL), F32),
        compiler_params=pltpu.CompilerParams(dimension_semantics=("arbitrary",),
                                             vmem_limit_bytes=VMEM_LIMIT),
        name="moe_experts",
    )(tile_expert, tile_valid, tile_src, xs, wg, wu, wd)


def _combine_body(pos_ref, x_ref, route_ref, ys_ref, out_ref, buf_ref, sem, *, tc):
    def start(r, _):
        for kk in range(2):
            p = pos_ref[0, 0, 2 * r + kk]
            _row_copy(ys_ref.at[pl.ds(p, 1)], buf_ref.at[kk, pl.ds(r, 1)], sem).start()
        return 0

    def wait(r, _):
        for kk in range(2):
            _row_copy(ys_ref.at[pl.ds(0, 1)], buf_ref.at[kk, pl.ds(0, 1)], sem).wait()
        return 0

    lax.fori_loop(0, tc, start, 0)
    lax.fori_loop(0, tc, wait, 0)
    route = route_ref[...]
    out_ref[...] = x_ref[...] + route[:, 0:1] * buf_ref[0] + route[:, 1:2] * buf_ref[1]


def _combine(pos, x, route, ys, tc):
    t = x.shape[0]
    return pl.pallas_call(
        functools.partial(_combine_body, tc=tc),
        grid=(t // tc,),
        in_specs=[pl.BlockSpec((1, 1, 2 * tc), lambda i: (i, 0, 0), memory_space=pltpu.SMEM),
                  pl.BlockSpec((tc, D_MODEL), lambda i: (i, 0)),
                  pl.BlockSpec((tc, LANES), lambda i: (i, 0)),
                  pl.BlockSpec(memory_space=pl.ANY)],
        out_specs=pl.BlockSpec((tc, D_MODEL), lambda i: (i, 0)),
        out_shape=jax.ShapeDtypeStruct((t, D_MODEL), F32),
        scratch_shapes=[pltpu.VMEM((2, tc, D_MODEL), F32), pltpu.SemaphoreType.DMA(())],
        compiler_params=pltpu.CompilerParams(dimension_semantics=("arbitrary",)),
        name="moe_combine",
    )(pos.reshape(t // tc, 1, 2 * tc), x, route, ys)


def _routing_plan(route, counts, n_tiles):
    cnt = counts[0, N_GROUPS:N_GROUPS + N_EXPERTS].astype(jnp.int32)
    padded = ((cnt + ROW_TILE - 1) // ROW_TILE) * ROW_TILE
    ends = jnp.cumsum(padded)
    offsets = ends - padded
    eid = route[:, 2:4].astype(jnp.int32)
    rank = route[:, 4:6].astype(jnp.int32)
    pos = offsets[eid] + rank

    used = ends[-1] // ROW_TILE
    tiles = jnp.arange(n_tiles, dtype=jnp.int32)
    valid = tiles < used
    last = jnp.maximum(used - 1, 0)
    expert_of = jnp.searchsorted(ends, tiles * ROW_TILE, side='right').astype(jnp.int32)
    tile_expert = jnp.where(valid, expert_of, expert_of[last])
    tile_src = jnp.minimum(tiles, last)
    last_tile = ends // ROW_TILE - 1
    has_pad = (cnt > 0) & (padded > cnt)
    zero_tiles = jnp.concatenate([jnp.where(has_pad, last_tile, -1).astype(jnp.int32),
                                  jnp.where(valid, -1, tiles)])
    order = jnp.argsort(zero_tiles < 0, stable=True)
    zero_tiles = zero_tiles[order]
    n_zero = jnp.sum(zero_tiles >= 0).astype(jnp.int32).reshape(1)
    return pos, tile_expert, valid.astype(jnp.int32), tile_src, jnp.maximum(zero_tiles, 0), n_zero


def _rope_tables(positions):
    half = ROPE // 2
    inv_freq = ROPE_THETA ** (-jnp.arange(half, dtype=F32) / half)
    ang = positions.astype(F32)[..., None] * inv_freq
    cos, sin = jnp.cos(ang), jnp.sin(ang)
    z = lambda n: jnp.zeros(ang.shape[:-1] + (n,), F32)
    cos_t = jnp.concatenate([jnp.ones(ang.shape[:-1] + (NOPE,), F32), cos, cos, z(LANES - QK_DIM)], -1)
    sin_m = jnp.concatenate([z(NOPE), -sin, z(half), z(LANES - QK_DIM)], -1)
    sin_p = jnp.concatenate([z(NOPE), z(half), sin, z(LANES - QK_DIM)], -1)
    flat = lambda a: a.reshape(-1, LANES)
    return flat(cos_t), flat(sin_m), flat(sin_p)


def _head_slots(w, width):
    k = w.shape[0]
    w = w.reshape(k, HEADS, width)
    return jnp.pad(w, ((0, 0), (0, 0), (0, LANES - width))).reshape(k, HEADS * LANES)


def _layer_weights(l, w_in, w_uq, w_ukv, q_head_g, k_head_g, w_router_group, w_router_expert):
    wi = w_in[l]
    kr = jnp.pad(wi[:, C_KR:C_KR + ROPE], ((0, 0), (NOPE, LANES - QK_DIM)))
    win = jnp.concatenate([wi[:, :C_KR], kr, wi[:, C_KR + ROPE:]], axis=1).astype(BF16)
    wuq = _head_slots(w_uq[l], QK_DIM).astype(BF16)
    kv = w_ukv[l].reshape(KV_LORA, HEADS, NOPE + V_DIM)
    wuk = _head_slots(kv[:, :, :NOPE].reshape(KV_LORA, HEADS * NOPE), NOPE)
    wuv = kv[:, :, NOPE:].reshape(KV_LORA, HEADS * V_DIM)
    wukv = jnp.concatenate([wuk, wuv], axis=1).astype(BF16)
    pad_g = lambda g: jnp.tile(jnp.pad(g.astype(F32), (0, LANES - QK_DIM)), HEADS)[None, :]
    gq = pad_g(q_head_g[l]) * (QK_DIM ** -0.5)
    gk = pad_g(k_head_g[l])
    wr = jnp.concatenate([w_router_group[l], w_router_expert[l]], axis=1).astype(F32)
    wr = jnp.pad(wr, ((0, 0), (0, LANES - wr.shape[1])))
    wrh = wr.astype(BF16)
    wrl = (wr - wrh.astype(F32)).astype(BF16)
    return win, wuq, wukv, gq, gk, wrh, wrl


def kernel(x, positions, norm_mix_g, w_in, q_lora_g, w_uq, kv_lora_g, w_ukv, q_head_g, k_head_g, w_attn_proj, lam_re, lam_im, log_dt, b_re, b_im, c_re, c_im, d_skip, w_glu, w_out, norm_ffn_g, w_router_group, w_router_expert, w_exp_gate, w_exp_up, w_exp_down):
    bsz, seq, _ = x.shape
    t = bsz * seq
    depth = w_in.shape[0]
    tm = 512 if t % 512 == 0 else 256
    tq = 256
    bb = 2 if bsz % 2 == 0 else 1
    n_tiles = (2 * t) // ROW_TILE + N_EXPERTS
    n_rows = n_tiles * ROW_TILE
    row1 = lambda g: g.astype(F32)[None, :]

    cos_t, sin_m, sin_p = _rope_tables(positions)
    xf = x.reshape(t, D_MODEL).astype(F32)
    for l in range(depth):
        win, wuq, wukv, gq, gk, wrh, wrl = _layer_weights(
            l, w_in, w_uq, w_ukv, q_head_g, k_head_g, w_router_group, w_router_expert)
        q, k, v, u, gates = _k1(xf, row1(norm_mix_g[l]), win, row1(q_lora_g[l]), wuq,
                                row1(kv_lora_g[l]), wukv, gq, gk, cos_t, sin_m, sin_p, tm)
        o = _attention(q.reshape(bsz, seq, -1), k.reshape(bsz, seq, -1), v.reshape(bsz, seq, -1), tq)
        ssm_w = _ssm_weights(lam_re[l], lam_im[l], log_dt[l], b_re[l], b_im[l], c_re[l], c_im[l], d_skip[l])
        ys = _ssm(u.reshape(bsz, seq, SSM_WIDTH), *ssm_w, bb)
        xn, h2, route, counts = _k4(xf, o.reshape(t, -1), ys.reshape(t, SSM_WIDTH), gates,
                                    w_attn_proj[l].astype(BF16), w_glu[l].astype(BF16),
                                    w_out[l].astype(BF16), row1(norm_ffn_g[l]), wrh, wrl, tm)
        pos, tile_expert, tile_valid, tile_src, zero_tiles, n_zero = _routing_plan(route, counts, n_tiles)
        xs = _dispatch(zero_tiles, n_zero, pos, h2, n_rows, tm)
        ye = _experts(tile_expert, tile_valid, tile_src, xs, w_exp_gate[l].astype(BF16),
                      w_exp_up[l].astype(BF16), w_exp_down[l].astype(BF16))
        xf = _combine(pos, xn, route, ye, tm)
    return xf.reshape(bsz, seq, D_MODEL).astype(x.dtype)
```

```python
import functools
import math

import jax
import jax.numpy as jnp
from jax import lax
from jax.experimental import pallas as pl
from jax.experimental.pallas import tpu as pltpu

F32 = jnp.float32
BF16 = jnp.bfloat16

D_MODEL = 1024
CHUNK = 64
HEADS = 8
NOPE = 64
ROPE = 32
QK_DIM = NOPE + ROPE
V_DIM = 64
Q_LORA = 384
KV_LORA = 256
ROPE_THETA = 10000.0
SSM_WIDTH = 512
SSM_GROUP = 16
SSM_GROUPS = 32
SSM_STATE = 64
N_GROUPS = 4
PER_GROUP = 8
N_EXPERTS = 32
D_EXPERT = 256
EPS = 1e-6

LANES = 128
SUBLANES = 8
SSM_CHUNK = 16
N_PAIR = SSM_CHUNK // 2
SSM_QBLOCKS = SSM_WIDTH // LANES
GROUPS_PER_QBLOCK = LANES // SSM_GROUP
STATE_COLS = GROUPS_PER_QBLOCK * 2 * SSM_STATE
ROW_TILE = 256
ROW_UNROLL = 8
VMEM_LIMIT = 56 * 1024 * 1024

C_Q = 0
C_KV = C_Q + Q_LORA
C_KR = C_KV + KV_LORA
C_U = C_KR + LANES
C_GATE = C_U + SSM_WIDTH
IN_COLS_PAD = C_GATE + 2 * D_MODEL


def _rms(x, g):
    return x * lax.rsqrt(jnp.mean(x * x, axis=-1, keepdims=True) + EPS) * g


def _dot(a, b):
    return jnp.dot(a, b, preferred_element_type=F32)


def _k1_body(x_ref, gmix_ref, win_ref, gql_ref, wuq_ref, gkvl_ref, wukv_ref, gq_ref, gk_ref,
             cos_ref, sin_ref, q_ref, k_ref, v_ref, u_ref, gates_ref):
    hb = _rms(x_ref[...], gmix_ref[...]).astype(BF16)

    def proj(lo, hi):
        return _dot(hb, win_ref[:, lo:hi])

    lane = lax.broadcasted_iota(jnp.int32, cos_ref.shape, 1)
    sin = sin_ref[...]
    cos_t = jnp.where(lane < NOPE, 1.0, jnp.where(lane < QK_DIM, cos_ref[...], 0.0))
    sin_m = jnp.where((lane >= NOPE) & (lane < NOPE + ROPE // 2), -sin, 0.0)
    sin_p = jnp.where((lane >= NOPE + ROPE // 2) & (lane < QK_DIM), sin, 0.0)

    def head_norm_rope(r, g):
        ms = jnp.sum(r * r, axis=-1, keepdims=True) * (1.0 / QK_DIM)
        rn = r * lax.rsqrt(ms + EPS) * g
        return (rn * cos_t + pltpu.roll(rn, LANES - ROPE // 2, 1) * sin_m
                + pltpu.roll(rn, ROPE // 2, 1) * sin_p)

    qn = _rms(proj(C_Q, C_KV), gql_ref[...]).astype(BF16)
    qf = _dot(qn, wuq_ref[...])
    for h in range(HEADS):
        sl = slice(h * LANES, (h + 1) * LANES)
        q_ref[:, sl] = head_norm_rope(qf[:, sl], gq_ref[:, sl]).astype(BF16)

    kvn = _rms(proj(C_KV, C_KR), gkvl_ref[...]).astype(BF16)
    kf = _dot(kvn, wukv_ref[...])
    kr = proj(C_KR, C_U)
    for h in range(HEADS):
        sl = slice(h * LANES, (h + 1) * LANES)
        k_ref[:, sl] = head_norm_rope(kf[:, sl] + kr, gk_ref[:, sl]).astype(BF16)
    vlane = lax.broadcasted_iota(jnp.int32, (1, HEADS * LANES), 1)
    ones_col = ((vlane % LANES) == V_DIM).astype(F32)
    v_ref[...] = (kf[:, HEADS * LANES:] + ones_col).astype(BF16)

    u_ref[...] = proj(C_U, C_GATE)
    gates_ref[...] = jax.nn.sigmoid(proj(C_GATE, IN_COLS_PAD)).astype(BF16)


def _k1(x, gmix, win, gql, wuq, gkvl, wukv, gq, gk, cos_t, sin_t, tm):
    t = x.shape[0]
    row = lambda w: pl.BlockSpec((tm, w), lambda i: (i, 0))
    full = lambda a: pl.BlockSpec(a.shape, lambda i: (0,) * a.ndim)
    return pl.pallas_call(
        _k1_body,
        grid=(t // tm,),
        in_specs=[row(D_MODEL), full(gmix), full(win), full(gql), full(wuq), full(gkvl), full(wukv),
                  full(gq), full(gk), row(LANES), row(LANES)],
        out_specs=[row(HEADS * LANES), row(HEADS * LANES), row(HEADS * LANES), row(SSM_WIDTH),
                   row(2 * D_MODEL)],
        out_shape=[jax.ShapeDtypeStruct((t, HEADS * LANES), BF16),
                   jax.ShapeDtypeStruct((t, HEADS * LANES), BF16),
                   jax.ShapeDtypeStruct((t, HEADS * LANES), BF16),
                   jax.ShapeDtypeStruct((t, SSM_WIDTH), F32),
                   jax.ShapeDtypeStruct((t, 2 * D_MODEL), BF16)],
        compiler_params=pltpu.CompilerParams(dimension_semantics=("arbitrary",),
                                             vmem_limit_bytes=VMEM_LIMIT),
        name="k1_inproj",
    )(x, gmix, win, gql, wuq, gkvl, wukv, gq, gk, cos_t, sin_t)


def _attn_body(q_ref, k_ref, v_ref, o_ref, *, tq, nq):
    i = pl.program_id(1)
    row_chunk = lax.broadcasted_iota(jnp.int32, (tq, tq), 0) // CHUNK
    col_chunk = lax.broadcasted_iota(jnp.int32, (tq, tq), 1) // CHUNK
    diag_mask = col_chunk <= row_chunk
    neg = jnp.float32(-1e30)

    def scores(hh, n_full):
        sl = slice(hh * LANES, (hh + 1) * LANES)
        s = lax.dot_general(q_ref[0, :, sl], k_ref[0, :n_full + tq, sl], (((1,), (1,)), ((), ())),
                            preferred_element_type=F32)
        s_diag = jnp.where(diag_mask, s[:, n_full:], neg)
        return jnp.concatenate([s[:, :n_full], s_diag], axis=1) if n_full else s_diag

    def probs(s):
        return jnp.exp2(s - jnp.max(s, axis=-1, keepdims=True)).astype(BF16)

    def values(p, hh, n_full):
        of = _dot(p, v_ref[0, :n_full + tq, hh * LANES:(hh + 1) * LANES])
        return of * (1.0 / of[:, V_DIM:V_DIM + 1])

    for ii in range(nq):
        @pl.when(i == ii)
        def _(ii=ii):
            n_full = ii * tq
            lane = lax.broadcasted_iota(jnp.int32, (tq, LANES), 1)
            ss = [scores(hh, n_full) for hh in range(HEADS)]
            ps = [probs(s) for s in ss]
            os_ = [values(p, hh, n_full) for hh, p in enumerate(ps)]
            for hp in range(HEADS // 2):
                o_ref[0, :, hp * LANES:(hp + 1) * LANES] = jnp.where(
                    lane < V_DIM, os_[2 * hp], pltpu.roll(os_[2 * hp + 1], V_DIM, 1)).astype(BF16)


def _attention(q, k, v, tq):
    b, l, _ = q.shape
    nq = l // tq
    return pl.pallas_call(
        functools.partial(_attn_body, tq=tq, nq=nq),
        grid=(b, nq),
        in_specs=[pl.BlockSpec((1, tq, HEADS * LANES), lambda bi, i: (bi, i, 0)),
                  pl.BlockSpec((1, l, HEADS * LANES), lambda bi, i: (bi, 0, 0)),
                  pl.BlockSpec((1, l, HEADS * LANES), lambda bi, i: (bi, 0, 0))],
        out_specs=pl.BlockSpec((1, tq, HEADS * V_DIM), lambda bi, i: (bi, i, 0)),
        out_shape=jax.ShapeDtypeStruct((b, l, HEADS * V_DIM), BF16),
        compiler_params=pltpu.CompilerParams(
            dimension_semantics=("arbitrary", "arbitrary"), vmem_limit_bytes=VMEM_LIMIT),
        name="attention",
    )(q, k, v)


def _cmul(a, b):
    return a[0] * b[0] - a[1] * b[1], a[0] * b[1] + a[1] * b[0]


def _ssm_tables(lam_re, lam_im, log_dt, b_re, b_im, c_re, c_im, d_skip):
    lc = SSM_CHUNK
    nq, gq = SSM_QBLOCKS, GROUPS_PER_QBLOCK
    lr, li = lam_re.astype(F32), lam_im.astype(F32)
    dt = jnp.exp(log_dt.astype(F32))[:, None]
    steps = jnp.arange(lc + 1, dtype=F32)[:, None, None]
    mag = jnp.exp(lr * dt * steps)
    pw = (mag * jnp.cos(li * dt * steps), mag * jnp.sin(li * dt * steps))
    num = (pw[0][1] - 1.0, pw[1][1])
    den = lr * lr + li * li
    ratio = ((num[0] * lr + num[1] * li) / den, (num[1] * lr - num[0] * li) / den)
    b_bar = _cmul((ratio[0][..., None], ratio[1][..., None]), (b_re.astype(F32), b_im.astype(F32)))
    c = (c_re.astype(F32), c_im.astype(F32))

    m = _cmul((pw[0][:lc, :, :, None], pw[1][:lc, :, :, None]), (b_bar[0][None], b_bar[1][None]))
    kern = jnp.einsum('gip,dgpj->dgij', c[0], m[0]) - jnp.einsum('gip,dgpj->dgij', c[1], m[1])
    kern = kern.at[0].add(jnp.eye(SSM_GROUP, dtype=F32)[None] * d_skip.astype(F32)[:, :, None])
    kpad = jnp.concatenate([jnp.zeros_like(kern[:1]), kern], axis=0)
    big_d = jnp.arange(N_PAIR)
    rows = []
    for s2 in range(2):
        cols = []
        for t2 in range(2):
            kd = kpad[2 * big_d + t2 - s2 + 1].reshape(N_PAIR, nq, gq, SSM_GROUP, SSM_GROUP)
            cols.append(jnp.transpose(kd, (1, 0, 4, 2, 3)))
        rows.append(jnp.stack(cols, axis=3))
    tab_t = jnp.stack(rows, axis=2).reshape(nq, N_PAIR, 2, SSM_GROUP, 2 * LANES)

    vin = _cmul((pw[0][:lc][::-1][..., None], pw[1][:lc][::-1][..., None]),
                (b_bar[0][None], b_bar[1][None]))
    vin = jnp.stack(vin, axis=2).reshape(N_PAIR, 2, nq, gq, 2, SSM_STATE, SSM_GROUP)
    tab_in = jnp.transpose(vin, (2, 0, 1, 6, 3, 4, 5)).reshape(nq, N_PAIR, 2, SSM_GROUP, STATE_COLS)

    cw = _cmul((c[0][None], c[1][None]), (pw[0][1:lc + 1][:, :, None, :], pw[1][1:lc + 1][:, :, None, :]))
    cw = jnp.stack([cw[0], -cw[1]], axis=0).reshape(2, N_PAIR, 2, nq, gq, SSM_GROUP, SSM_STATE)
    tab_out = jnp.transpose(cw, (3, 1, 0, 6, 2, 4, 5)).reshape(nq, N_PAIR, 2 * SSM_STATE, 2 * LANES)

    ar = pw[0][lc].reshape(nq, gq, 1, SSM_STATE)
    ai = pw[1][lc].reshape(nq, gq, 1, SSM_STATE)
    a_same = jnp.concatenate([ar, ar], axis=2).reshape(nq, 1, STATE_COLS)
    a_swap = jnp.concatenate([-ai, ai], axis=2).reshape(nq, 1, STATE_COLS)
    return tab_t, tab_in, tab_out, jnp.concatenate([a_same, a_swap], axis=1)


def _ssm_body(u_ref, tt_ref, tin_ref, tout_ref, a_ref, y_ref, wt_ref, win_ref, wout_ref, st_ref, hp_ref,
              *, bb, nc):
    gq = GROUPS_PER_QBLOCK

    @pl.when(pl.program_id(1) == 0)
    def _():
        grp_t = (lax.broadcasted_iota(jnp.int32, (SSM_GROUP, 2 * LANES), 1) % LANES) // SSM_GROUP
        grp_in = lax.broadcasted_iota(jnp.int32, (SSM_GROUP, STATE_COLS), 1) // LANES
        grp_out = (lax.broadcasted_iota(jnp.int32, (LANES, 2 * LANES), 1) % LANES) // SSM_GROUP
        for d in range(N_PAIR):
            for s2 in range(2):
                piece_t = tt_ref[0, d, s2]
                piece_in = tin_ref[0, d, s2]
                for gl in range(gq):
                    rows = pl.ds(s2 * LANES + gl * SSM_GROUP, SSM_GROUP)
                    wt_ref[d, rows, :] = jnp.where(grp_t == gl, piece_t, 0.0).astype(BF16)
                    win_ref[d, rows, :] = jnp.where(grp_in == gl, piece_in, 0.0).astype(BF16)
            piece_out = tout_ref[0, d]
            for gl in range(gq):
                wout_ref[d, pl.ds(gl * LANES, LANES), :] = jnp.where(grp_out == gl, piece_out, 0.0).astype(BF16)

    xs = []
    for s in range(N_PAIR):
        per_batch = []
        for b in range(bb):
            x0 = u_ref[b, pl.ds(2 * s, nc, stride=SSM_CHUNK), :]
            x1 = u_ref[b, pl.ds(2 * s + 1, nc, stride=SSM_CHUNK), :]
            per_batch.append(jnp.concatenate([x0, x1], axis=1))
        xs.append(jnp.concatenate(per_batch, axis=0).astype(BF16))

    st = _dot(xs[0], win_ref[0])
    for s in range(1, N_PAIR):
        st = st + _dot(xs[s], win_ref[s])
    st_ref[...] = st

    a_same = a_ref[0, 0:1, :]
    a_swap = a_ref[0, 1:2, :]

    def step(c, carry):
        new = []
        for b in range(bb):
            h = carry[b]
            row = b * nc + c
            hp_ref[pl.ds(row, 1), :] = h
            swapped = jnp.concatenate(
                [pltpu.roll(h[:, g * LANES:(g + 1) * LANES], SSM_STATE, 1) for g in range(gq)], axis=1)
            new.append(a_same * h + a_swap * swapped + st_ref[pl.ds(row, 1), :])
        return tuple(new)

    lax.fori_loop(0, nc, step, tuple(jnp.zeros((1, STATE_COLS), F32) for _ in range(bb)))

    hp = hp_ref[...].astype(BF16)
    for t in range(N_PAIR):
        acc = _dot(hp, wout_ref[t])
        for s in range(t + 1):
            acc = acc + _dot(xs[s], wt_ref[t - s])
        yg = jax.nn.gelu(acc)
        for b in range(bb):
            rows = slice(b * nc, (b + 1) * nc)
            y_ref[b, pl.ds(2 * t, nc, stride=SSM_CHUNK), :] = yg[rows, :LANES]
            y_ref[b, pl.ds(2 * t + 1, nc, stride=SSM_CHUNK), :] = yg[rows, LANES:]


def _ssm(u, tab_t, tab_in, tab_out, a_chunk, bb):
    b, l, _ = u.shape
    nc = l // SSM_CHUNK
    wspec = lambda a: pl.BlockSpec((1,) + a.shape[1:], lambda q, bi: (q,) + (0,) * (a.ndim - 1))
    return pl.pallas_call(
        functools.partial(_ssm_body, bb=bb, nc=nc),
        grid=(SSM_QBLOCKS, b // bb),
        in_specs=[pl.BlockSpec((bb, l, LANES), lambda q, bi: (bi, 0, q)),
                  wspec(tab_t), wspec(tab_in), wspec(tab_out), wspec(a_chunk)],
        out_specs=pl.BlockSpec((bb, l, LANES), lambda q, bi: (bi, 0, q)),
        out_shape=jax.ShapeDtypeStruct((b, l, SSM_WIDTH), F32),
        scratch_shapes=[pltpu.VMEM((N_PAIR, 2 * LANES, 2 * LANES), BF16),
                        pltpu.VMEM((N_PAIR, 2 * LANES, STATE_COLS), BF16),
                        pltpu.VMEM((N_PAIR, STATE_COLS, 2 * LANES), BF16),
                        pltpu.VMEM((bb * nc, STATE_COLS), F32),
                        pltpu.VMEM((bb * nc, STATE_COLS), F32)],
        compiler_params=pltpu.CompilerParams(dimension_semantics=("arbitrary", "arbitrary"),
                                             vmem_limit_bytes=VMEM_LIMIT),
        name="ssm",
    )(u, tab_t, tab_in, tab_out, a_chunk)


def _k4_body(x_ref, o_ref, ys_ref, g_ref, wap_ref, wglu_ref, wout_ref, gffn_ref, wrh_ref, wrl_ref,
             xn_ref, h2_ref, route_ref, routet_ref, cnt_ref, carry_ref, *, tm):
    @pl.when(pl.program_id(0) == 0)
    def _():
        carry_ref[...] = jnp.zeros_like(carry_ref)

    y_attn = _dot(o_ref[...], wap_ref[...])
    vg = _dot(ys_ref[...].astype(BF16), wglu_ref[...])
    y_ssm = vg[:, :D_MODEL] * jax.nn.sigmoid(vg[:, D_MODEL:])
    g = g_ref[...].astype(F32)
    merged = g[:, :D_MODEL] * y_attn + g[:, D_MODEL:] * y_ssm
    xn = x_ref[...] + _dot(merged.astype(BF16), wout_ref[...])
    xn_ref[...] = xn
    h2 = _rms(xn, gffn_ref[...])
    h2_ref[...] = h2

    hh = h2.astype(BF16)
    hl = (h2 - hh.astype(F32)).astype(BF16)
    logits = _dot(hh, wrh_ref[...]) + _dot(hh, wrl_ref[...]) + _dot(hl, wrh_ref[...])

    lane = lax.broadcasted_iota(jnp.int32, (tm, LANES), 1).astype(F32)
    ninf = jnp.float32(-jnp.inf)
    big = jnp.float32(4 * LANES)

    def top(mask):
        val = jnp.max(jnp.where(mask, logits, ninf), axis=-1, keepdims=True)
        idx = jnp.min(jnp.where(mask & (logits == val), lane, big), axis=-1, keepdims=True)
        return val, idx

    gmask = lane < N_GROUPS
    gmax, gidx = top(gmask)
    g_w = 1.0 / jnp.sum(jnp.where(gmask, jnp.exp(logits - gmax), 0.0), axis=-1, keepdims=True)
    lo = N_GROUPS + PER_GROUP * gidx
    emask = (lane >= lo) & (lane < lo + PER_GROUP)
    l1, i1 = top(emask)
    l2, i2 = top(emask & (lane != i1))
    e21 = jnp.exp(l2 - l1)
    w1 = g_w / (1.0 + e21)
    w2 = g_w * e21 / (1.0 + e21)

    hot1 = lane == i1
    hot2 = lane == i2
    onehot = (hot1 | hot2).astype(BF16)
    tri = (lax.broadcasted_iota(jnp.int32, (tm, tm), 1)
           < lax.broadcasted_iota(jnp.int32, (tm, tm), 0)).astype(BF16)
    before = _dot(tri, onehot) + carry_ref[...]
    r1 = jnp.sum(jnp.where(hot1, before, 0.0), axis=-1, keepdims=True)
    r2 = jnp.sum(jnp.where(hot2, before, 0.0), axis=-1, keepdims=True)
    carry_ref[...] += jnp.sum(onehot.astype(F32), axis=0, keepdims=True)
    cnt_ref[...] = carry_ref[...]

    cols = (w1, w2, i1 - N_GROUPS, i2 - N_GROUPS, r1, r2)
    out = jnp.zeros((tm, LANES), F32)
    for n, col in enumerate(cols):
        out = jnp.where(lane == n, col, out)
    route_ref[...] = out
    routet_ref[...] = out.T[:SUBLANES, :]


def _k4(x, o, ys, gates, wap, wglu, wout, gffn, wrh, wrl, tm):
    t = x.shape[0]
    row = lambda w: pl.BlockSpec((tm, w), lambda i: (i, 0))
    full = lambda a: pl.BlockSpec(a.shape, lambda i: (0,) * a.ndim)
    return pl.pallas_call(
        functools.partial(_k4_body, tm=tm),
        grid=(t // tm,),
        in_specs=[row(D_MODEL), row(HEADS * V_DIM), row(SSM_WIDTH), row(2 * D_MODEL),
                  full(wap), full(wglu), full(wout), full(gffn), full(wrh), full(wrl)],
        out_specs=[row(D_MODEL), row(D_MODEL), row(LANES), pl.BlockSpec((SUBLANES, tm), lambda i: (0, i)),
                   pl.BlockSpec((1, LANES), lambda i: (0, 0))],
        out_shape=[jax.ShapeDtypeStruct((t, D_MODEL), F32), jax.ShapeDtypeStruct((t, D_MODEL), F32),
                   jax.ShapeDtypeStruct((t, LANES), F32), jax.ShapeDtypeStruct((SUBLANES, t), F32),
                   jax.ShapeDtypeStruct((1, LANES), F32)],
        scratch_shapes=[pltpu.VMEM((1, LANES), F32)],
        compiler_params=pltpu.CompilerParams(dimension_semantics=("arbitrary",),
                                             vmem_limit_bytes=VMEM_LIMIT),
        name="k4_merge_router",
    )(x, o, ys, gates, wap, wglu, wout, gffn, wrh, wrl)


def _row_copy(src, dst, sem):
    return pltpu.make_async_copy(src, dst, sem)


def _dispatch_body(zt_ref, nz_ref, pos_ref, h2_ref, xs_ref, zeros_ref, sem, *, td):
    @pl.when(pl.program_id(0) == 0)
    def _():
        zeros_ref[...] = jnp.zeros_like(zeros_ref)
        n = nz_ref[0]

        def zstart(z, _):
            _row_copy(zeros_ref, xs_ref.at[pl.ds(pl.multiple_of(zt_ref[z] * ROW_TILE, ROW_TILE), ROW_TILE)],
                      sem).start()
            return 0

        def zwait(z, _):
            _row_copy(zeros_ref, xs_ref.at[pl.ds(0, ROW_TILE)], sem).wait()
            return 0

        lax.fori_loop(0, n, zstart, 0)
        lax.fori_loop(0, n, zwait, 0)

    def start(r, _):
        for kk in range(2):
            p = pos_ref[0, kk, r]
            _row_copy(h2_ref.at[pl.ds(r, 1)], xs_ref.at[pl.ds(p, 1)], sem).start(priority=kk)
        return 0

    lax.fori_loop(0, td, start, 0, unroll=ROW_UNROLL)
    for kk in range(2):
        _row_copy(h2_ref, xs_ref.at[pl.ds(0, td)], sem).wait()


def _dispatch(zero_tiles, n_zero, pos, h2, n_rows, td):
    t = h2.shape[0]
    grid_spec = pltpu.PrefetchScalarGridSpec(
        num_scalar_prefetch=2,
        grid=(t // td,),
        in_specs=[pl.BlockSpec((1, 2, td), lambda i, zt, nz: (i, 0, 0), memory_space=pltpu.SMEM),
                  pl.BlockSpec((td, D_MODEL), lambda i, zt, nz: (i, 0))],
        out_specs=pl.BlockSpec(memory_space=pl.ANY),
        scratch_shapes=[pltpu.VMEM((ROW_TILE, D_MODEL), F32), pltpu.SemaphoreType.DMA(())],
    )
    return pl.pallas_call(
        functools.partial(_dispatch_body, td=td),
        grid_spec=grid_spec,
        out_shape=jax.ShapeDtypeStruct((n_rows, D_MODEL), F32),
        compiler_params=pltpu.CompilerParams(dimension_semantics=("arbitrary",)),
        name="moe_dispatch",
    )(zero_tiles, n_zero, pos, h2)


def _expert_body(te_ref, tv_ref, xi_ref, xs_ref, wg_ref, wu_ref, wd_ref, ys_ref, wgb_ref, wub_ref, wdb_ref):
    i = pl.program_id(0)
    prev = te_ref[jnp.maximum(i - 1, 0)]

    @pl.when((i == 0) | (te_ref[i] != prev))
    def _():
        wgb_ref[...] = wg_ref[0].astype(BF16)
        wub_ref[...] = wu_ref[0].astype(BF16)
        wdb_ref[...] = wd_ref[0].astype(BF16)

    @pl.when(tv_ref[i] == 1)
    def _():
        xb = xs_ref[...].astype(BF16)
        hidden = jax.nn.silu(_dot(xb, wgb_ref[...])) * _dot(xb, wub_ref[...])
        ys_ref[...] = _dot(hidden.astype(BF16), wdb_ref[...])

    @pl.when(tv_ref[i] == 0)
    def _():
        ys_ref[...] = jnp.zeros_like(ys_ref)


def _experts(tile_expert, tile_valid, tile_src, xs, wg, wu, wd):
    n_rows = xs.shape[0]
    grid_spec = pltpu.PrefetchScalarGridSpec(
        num_scalar_prefetch=3,
        grid=(n_rows // ROW_TILE,),
        in_specs=[pl.BlockSpec((ROW_TILE, D_MODEL), lambda i, te, tv, xi: (xi[i], 0)),
                  pl.BlockSpec((1, D_MODEL, D_EXPERT), lambda i, te, tv, xi: (te[i], 0, 0)),
                  pl.BlockSpec((1, D_MODEL, D_EXPERT), lambda i, te, tv, xi: (te[i], 0, 0)),
                  pl.BlockSpec((1, D_EXPERT, D_MODEL), lambda i, te, tv, xi: (te[i], 0, 0))],
        out_specs=pl.BlockSpec((ROW_TILE, D_MODEL), lambda i, te, tv, xi: (i, 0)),
        scratch_shapes=[pltpu.VMEM((D_MODEL, D_EXPERT), BF16), pltpu.VMEM((D_MODEL, D_EXPERT), BF16),
                        pltpu.VMEM((D_EXPERT, D_MODEL), BF16)],
    )
    return pl.pallas_call(
        _expert_body,
        grid_spec=grid_spec,
        out_shape=jax.ShapeDtypeStruct((n_rows, D_MODEL), F32),
        compiler_params=pltpu.CompilerParams(dimension_semantics=("arbitrary",),
                                             vmem_limit_bytes=VMEM_LIMIT),
        name="moe_experts",
    )(tile_expert, tile_valid, tile_src, xs, wg, wu, wd)


def _combine_body(pos_ref, x_ref, route_ref, ys_ref, out_ref, buf_ref, sem, *, tc):
    def start(r, _):
        for kk in range(2):
            p = pos_ref[0, kk, r]
            _row_copy(ys_ref.at[pl.ds(p, 1)], buf_ref.at[kk, pl.ds(r, 1)], sem).start(priority=kk)
        return 0

    lax.fori_loop(0, tc, start, 0, unroll=ROW_UNROLL)
    for kk in range(2):
        _row_copy(ys_ref.at[pl.ds(0, tc)], buf_ref.at[kk], sem).wait()
    route = route_ref[...]
    out_ref[...] = x_ref[...] + route[:, 0:1] * buf_ref[0] + route[:, 1:2] * buf_ref[1]


def _combine(pos, x, route, ys, tc):
    t = x.shape[0]
    return pl.pallas_call(
        functools.partial(_combine_body, tc=tc),
        grid=(t // tc,),
        in_specs=[pl.BlockSpec((1, 2, tc), lambda i: (i, 0, 0), memory_space=pltpu.SMEM),
                  pl.BlockSpec((tc, D_MODEL), lambda i: (i, 0)),
                  pl.BlockSpec((tc, LANES), lambda i: (i, 0)),
                  pl.BlockSpec(memory_space=pl.ANY)],
        out_specs=pl.BlockSpec((tc, D_MODEL), lambda i: (i, 0)),
        out_shape=jax.ShapeDtypeStruct((t, D_MODEL), F32),
        scratch_shapes=[pltpu.VMEM((2, tc, D_MODEL), F32), pltpu.SemaphoreType.DMA(())],
        compiler_params=pltpu.CompilerParams(dimension_semantics=("arbitrary",)),
        name="moe_combine",
    )(pos, x, route, ys)


def _routing_plan(route_t, counts, n_tiles, td):
    t = route_t.shape[1]
    cnt = counts[0, N_GROUPS:N_GROUPS + N_EXPERTS].astype(jnp.int32)
    padded = ((cnt + ROW_TILE - 1) // ROW_TILE) * ROW_TILE
    ends = jnp.cumsum(padded)
    offsets = ends - padded
    experts = jnp.arange(N_EXPERTS, dtype=jnp.int32)[:, None]

    def position(e_row, r_row):
        e = e_row.astype(jnp.int32)[None, :]
        return jnp.sum(jnp.where(e == experts, offsets[:, None], 0), axis=0) + r_row.astype(jnp.int32)

    pos = jnp.stack([position(route_t[2], route_t[4]).reshape(t // td, td),
                     position(route_t[3], route_t[5]).reshape(t // td, td)], axis=1)

    used = ends[-1] // ROW_TILE
    tiles = jnp.arange(n_tiles, dtype=jnp.int32)
    valid = tiles < used
    last = jnp.maximum(used - 1, 0)
    expert_of = jnp.sum((ends[None, :] <= (tiles * ROW_TILE)[:, None]).astype(jnp.int32), axis=1)
    tile_expert = jnp.where(valid, expert_of, jnp.sum(jnp.where(tiles == last, expert_of, 0)))
    tile_src = jnp.minimum(tiles, last)
    last_tile = ends // ROW_TILE - 1
    has_pad = (cnt > 0) & (padded > cnt)
    zero_tiles = jnp.concatenate([jnp.where(has_pad, last_tile, -1).astype(jnp.int32),
                                  jnp.where(valid, -1, tiles)])
    order = jnp.argsort(zero_tiles < 0, stable=True)
    zero_tiles = zero_tiles[order]
    n_zero = jnp.sum(zero_tiles >= 0).astype(jnp.int32).reshape(1)
    return pos, tile_expert, valid.astype(jnp.int32), tile_src, jnp.maximum(zero_tiles, 0), n_zero


def _rope_tables(positions):
    half = ROPE // 2
    inv_freq = ROPE_THETA ** (-jnp.arange(half, dtype=F32) / half)
    ang = positions.astype(F32).reshape(-1, 1) * jnp.tile(inv_freq, LANES // half)[None, :]
    return jnp.cos(ang), jnp.sin(ang)


def _head_slots(w, width):
    k = w.shape[0]
    w = w.reshape(k, HEADS, width)
    return jnp.pad(w, ((0, 0), (0, 0), (0, LANES - width))).reshape(k, HEADS * LANES)


def _layer_weights(l, w_in, w_uq, w_ukv, q_head_g, k_head_g, w_router_group, w_router_expert):
    wi = w_in[l]
    kr = jnp.pad(wi[:, C_KR:C_KR + ROPE], ((0, 0), (NOPE, LANES - QK_DIM)))
    win = jnp.concatenate([wi[:, :C_KR], kr, wi[:, C_KR + ROPE:]], axis=1).astype(BF16)
    wuq = _head_slots(w_uq[l], QK_DIM).astype(BF16)
    kv = w_ukv[l].reshape(KV_LORA, HEADS, NOPE + V_DIM)
    wuk = _head_slots(kv[:, :, :NOPE].reshape(KV_LORA, HEADS * NOPE), NOPE)
    wuv = _head_slots(kv[:, :, NOPE:].reshape(KV_LORA, HEADS * V_DIM), V_DIM)
    wukv = jnp.concatenate([wuk, wuv], axis=1).astype(BF16)
    pad_g = lambda g: jnp.tile(jnp.pad(g.astype(F32), (0, LANES - QK_DIM)), HEADS)[None, :]
    gq = pad_g(q_head_g[l]) * (QK_DIM ** -0.5 * math.log2(math.e))
    gk = pad_g(k_head_g[l])
    wr = jnp.concatenate([w_router_group[l], w_router_expert[l]], axis=1).astype(F32)
    wr = jnp.pad(wr, ((0, 0), (0, LANES - wr.shape[1])))
    wrh = wr.astype(BF16)
    wrl = (wr - wrh.astype(F32)).astype(BF16)
    return win, wuq, wukv, gq, gk, wrh, wrl


def kernel(x, positions, norm_mix_g, w_in, q_lora_g, w_uq, kv_lora_g, w_ukv, q_head_g, k_head_g, w_attn_proj, lam_re, lam_im, log_dt, b_re, b_im, c_re, c_im, d_skip, w_glu, w_out, norm_ffn_g, w_router_group, w_router_expert, w_exp_gate, w_exp_up, w_exp_down):
    bsz, seq, _ = x.shape
    t = bsz * seq
    depth = w_in.shape[0]
    tm = 512 if t % 512 == 0 else 256
    tq = 256
    bb = 2 if bsz % 2 == 0 else 1
    n_tiles = (2 * t) // ROW_TILE + N_EXPERTS
    n_rows = n_tiles * ROW_TILE
    row1 = lambda g: g.astype(F32)[None, :]

    cos_t, sin_t = _rope_tables(positions)
    xf = x.reshape(t, D_MODEL).astype(F32)
    for l in range(depth):
        win, wuq, wukv, gq, gk, wrh, wrl = _layer_weights(
            l, w_in, w_uq, w_ukv, q_head_g, k_head_g, w_router_group, w_router_expert)
        q, k, v, u, gates = _k1(xf, row1(norm_mix_g[l]), win, row1(q_lora_g[l]), wuq,
                                row1(kv_lora_g[l]), wukv, gq, gk, cos_t, sin_t, tm)
        o = _attention(q.reshape(bsz, seq, -1), k.reshape(bsz, seq, -1), v.reshape(bsz, seq, -1), tq)
        tables = _ssm_tables(lam_re[l], lam_im[l], log_dt[l], b_re[l], b_im[l], c_re[l], c_im[l], d_skip[l])
        ys = _ssm(u.reshape(bsz, seq, SSM_WIDTH), *tables, bb)
        xn, h2, route, route_t, counts = _k4(xf, o.reshape(t, -1), ys.reshape(t, SSM_WIDTH), gates,
                                             w_attn_proj[l].astype(BF16), w_glu[l].astype(BF16),
                                             w_out[l].astype(BF16), row1(norm_ffn_g[l]), wrh, wrl, tm)
        pos, tile_expert, tile_valid, tile_src, zero_tiles, n_zero = _routing_plan(route_t, counts, n_tiles, tm)
        xs = _dispatch(zero_tiles, n_zero, pos, h2, n_rows, tm)
        ye = _experts(tile_expert, tile_valid, tile_src, xs, w_exp_gate[l], w_exp_up[l], w_exp_down[l])
        xf = _combine(pos, xn, route, ye, tm)
    return xf.reshape(bsz, seq, D_MODEL).astype(x.dtype)
```

```python
import functools
import math

import jax
import jax.numpy as jnp
from jax import lax
from jax.experimental import pallas as pl
from jax.experimental.pallas import tpu as pltpu

F32 = jnp.float32
BF16 = jnp.bfloat16

D_MODEL = 1024
CHUNK = 64
HEADS = 8
NOPE = 64
ROPE = 32
QK_DIM = NOPE + ROPE
V_DIM = 64
Q_LORA = 384
KV_LORA = 256
ROPE_THETA = 10000.0
SSM_WIDTH = 512
SSM_GROUP = 16
SSM_GROUPS = 32
SSM_STATE = 64
N_GROUPS = 4
PER_GROUP = 8
N_EXPERTS = 32
D_EXPERT = 256
EPS = 1e-6

LANES = 128
SUBLANES = 8
SSM_CHUNK = 16
N_PAIR = SSM_CHUNK // 2
SSM_QBLOCKS = SSM_WIDTH // LANES
GROUPS_PER_QBLOCK = LANES // SSM_GROUP
STATE_COLS = GROUPS_PER_QBLOCK * 2 * SSM_STATE
ROW_TILE = 256
ROW_UNROLL = 8
ROW_BLOCKS = D_MODEL // LANES
VMEM_LIMIT = 56 * 1024 * 1024

C_Q = 0
C_KV = C_Q + Q_LORA
C_KR = C_KV + KV_LORA
C_U = C_KR + LANES
C_GATE = C_U + SSM_WIDTH
IN_COLS_PAD = C_GATE + 2 * D_MODEL


def _rms(x, g):
    return x * lax.rsqrt(jnp.mean(x * x, axis=-1, keepdims=True) + EPS) * g


def _dot(a, b):
    return jnp.dot(a, b, preferred_element_type=F32)


def _to_token_tiles(ref, x, n):
    for k in range(ROW_BLOCKS):
        ref[pl.ds(k, n, stride=ROW_BLOCKS), :] = x[:, k * LANES:(k + 1) * LANES]


def _from_token_tiles(ref, n):
    return jnp.concatenate([ref[pl.ds(k, n, stride=ROW_BLOCKS), :] for k in range(ROW_BLOCKS)], axis=1)


def _k1_body(x_ref, gmix_ref, win_ref, gql_ref, wuq_ref, gkvl_ref, wukv_ref, gq_ref, gk_ref,
             cos_ref, sin_ref, q_ref, k_ref, v_ref, u_ref, gates_ref):
    hb = _rms(x_ref[...], gmix_ref[...]).astype(BF16)

    def proj(lo, hi):
        return _dot(hb, win_ref[:, lo:hi])

    lane = lax.broadcasted_iota(jnp.int32, cos_ref.shape, 1)
    sin = sin_ref[...]
    cos_t = jnp.where(lane < NOPE, 1.0, jnp.where(lane < QK_DIM, cos_ref[...], 0.0))
    sin_m = jnp.where((lane >= NOPE) & (lane < NOPE + ROPE // 2), -sin, 0.0)
    sin_p = jnp.where((lane >= NOPE + ROPE // 2) & (lane < QK_DIM), sin, 0.0)

    def head_norm_rope(r, g):
        ms = jnp.sum(r * r, axis=-1, keepdims=True) * (1.0 / QK_DIM)
        rn = r * lax.rsqrt(ms + EPS) * g
        return (rn * cos_t + pltpu.roll(rn, LANES - ROPE // 2, 1) * sin_m
                + pltpu.roll(rn, ROPE // 2, 1) * sin_p)

    qn = _rms(proj(C_Q, C_KV), gql_ref[...]).astype(BF16)
    qf = _dot(qn, wuq_ref[...])
    for h in range(HEADS):
        sl = slice(h * LANES, (h + 1) * LANES)
        q_ref[:, sl] = head_norm_rope(qf[:, sl], gq_ref[:, sl]).astype(BF16)

    kvn = _rms(proj(C_KV, C_KR), gkvl_ref[...]).astype(BF16)
    kf = _dot(kvn, wukv_ref[...])
    kr = proj(C_KR, C_U)
    for h in range(HEADS):
        sl = slice(h * LANES, (h + 1) * LANES)
        k_ref[:, sl] = head_norm_rope(kf[:, sl] + kr, gk_ref[:, sl]).astype(BF16)
    vlane = lax.broadcasted_iota(jnp.int32, (1, HEADS * LANES), 1)
    ones_col = ((vlane % LANES) == V_DIM).astype(F32)
    v_ref[...] = (kf[:, HEADS * LANES:] + ones_col).astype(BF16)

    u_ref[...] = proj(C_U, C_GATE)
    gates_ref[...] = jax.nn.sigmoid(proj(C_GATE, IN_COLS_PAD)).astype(BF16)


def _k1(x, gmix, win, gql, wuq, gkvl, wukv, gq, gk, cos_t, sin_t, tm):
    t = x.shape[0]
    row = lambda w: pl.BlockSpec((tm, w), lambda i: (i, 0))
    full = lambda a: pl.BlockSpec(a.shape, lambda i: (0,) * a.ndim)
    return pl.pallas_call(
        _k1_body,
        grid=(t // tm,),
        in_specs=[row(D_MODEL), full(gmix), full(win), full(gql), full(wuq), full(gkvl), full(wukv),
                  full(gq), full(gk), row(LANES), row(LANES)],
        out_specs=[row(HEADS * LANES), row(HEADS * LANES), row(HEADS * LANES), row(SSM_WIDTH),
                   row(2 * D_MODEL)],
        out_shape=[jax.ShapeDtypeStruct((t, HEADS * LANES), BF16),
                   jax.ShapeDtypeStruct((t, HEADS * LANES), BF16),
                   jax.ShapeDtypeStruct((t, HEADS * LANES), BF16),
                   jax.ShapeDtypeStruct((t, SSM_WIDTH), F32),
                   jax.ShapeDtypeStruct((t, 2 * D_MODEL), BF16)],
        compiler_params=pltpu.CompilerParams(dimension_semantics=("arbitrary",),
                                             vmem_limit_bytes=VMEM_LIMIT),
        name="k1_inproj",
    )(x, gmix, win, gql, wuq, gkvl, wukv, gq, gk, cos_t, sin_t)


def _attn_body(q_ref, k_ref, v_ref, o_ref, *, tq, nq):
    i = pl.program_id(1)
    row_chunk = lax.broadcasted_iota(jnp.int32, (tq, tq), 0) // CHUNK
    col_chunk = lax.broadcasted_iota(jnp.int32, (tq, tq), 1) // CHUNK
    diag_mask = col_chunk <= row_chunk
    neg = jnp.float32(-1e30)

    def scores(hh, n_full):
        sl = slice(hh * LANES, (hh + 1) * LANES)
        s = lax.dot_general(q_ref[0, :, sl], k_ref[0, :n_full + tq, sl], (((1,), (1,)), ((), ())),
                            preferred_element_type=F32)
        s_diag = jnp.where(diag_mask, s[:, n_full:], neg)
        return jnp.concatenate([s[:, :n_full], s_diag], axis=1) if n_full else s_diag

    def probs(s):
        return jnp.exp2(s - jnp.max(s, axis=-1, keepdims=True)).astype(BF16)

    def values(p, hh, n_full):
        of = _dot(p, v_ref[0, :n_full + tq, hh * LANES:(hh + 1) * LANES])
        return of * (1.0 / of[:, V_DIM:V_DIM + 1])

    for ii in range(nq):
        @pl.when(i == ii)
        def _(ii=ii):
            n_full = ii * tq
            lane = lax.broadcasted_iota(jnp.int32, (tq, LANES), 1)
            ss = [scores(hh, n_full) for hh in range(HEADS)]
            ps = [probs(s) for s in ss]
            os_ = [values(p, hh, n_full) for hh, p in enumerate(ps)]
            for hp in range(HEADS // 2):
                o_ref[0, :, hp * LANES:(hp + 1) * LANES] = jnp.where(
                    lane < V_DIM, os_[2 * hp], pltpu.roll(os_[2 * hp + 1], V_DIM, 1)).astype(BF16)


def _attention(q, k, v, tq):
    b, l, _ = q.shape
    nq = l // tq
    return pl.pallas_call(
        functools.partial(_attn_body, tq=tq, nq=nq),
        grid=(b, nq),
        in_specs=[pl.BlockSpec((1, tq, HEADS * LANES), lambda bi, i: (bi, i, 0)),
                  pl.BlockSpec((1, l, HEADS * LANES), lambda bi, i: (bi, 0, 0)),
                  pl.BlockSpec((1, l, HEADS * LANES), lambda bi, i: (bi, 0, 0))],
        out_specs=pl.BlockSpec((1, tq, HEADS * V_DIM), lambda bi, i: (bi, i, 0)),
        out_shape=jax.ShapeDtypeStruct((b, l, HEADS * V_DIM), BF16),
        compiler_params=pltpu.CompilerParams(
            dimension_semantics=("arbitrary", "arbitrary"), vmem_limit_bytes=VMEM_LIMIT),
        name="attention",
    )(q, k, v)


def _cmul(a, b):
    return a[0] * b[0] - a[1] * b[1], a[0] * b[1] + a[1] * b[0]


def _ssm_tables(lam_re, lam_im, log_dt, b_re, b_im, c_re, c_im, d_skip):
    lc = SSM_CHUNK
    nq, gq = SSM_QBLOCKS, GROUPS_PER_QBLOCK
    lr, li = lam_re.astype(F32), lam_im.astype(F32)
    dt = jnp.exp(log_dt.astype(F32))[:, None]
    steps = jnp.arange(lc + 1, dtype=F32)[:, None, None]
    mag = jnp.exp(lr * dt * steps)
    pw = (mag * jnp.cos(li * dt * steps), mag * jnp.sin(li * dt * steps))
    num = (pw[0][1] - 1.0, pw[1][1])
    den = lr * lr + li * li
    ratio = ((num[0] * lr + num[1] * li) / den, (num[1] * lr - num[0] * li) / den)
    b_bar = _cmul((ratio[0][..., None], ratio[1][..., None]), (b_re.astype(F32), b_im.astype(F32)))
    c = (c_re.astype(F32), c_im.astype(F32))

    m = _cmul((pw[0][:lc, :, :, None], pw[1][:lc, :, :, None]), (b_bar[0][None], b_bar[1][None]))
    kern = jnp.einsum('gip,dgpj->dgij', c[0], m[0]) - jnp.einsum('gip,dgpj->dgij', c[1], m[1])
    kern = kern.at[0].add(jnp.eye(SSM_GROUP, dtype=F32)[None] * d_skip.astype(F32)[:, :, None])
    kpad = jnp.concatenate([jnp.zeros_like(kern[:1]), kern], axis=0)
    big_d = jnp.arange(N_PAIR)
    rows = []
    for s2 in range(2):
        cols = []
        for t2 in range(2):
            kd = kpad[2 * big_d + t2 - s2 + 1].reshape(N_PAIR, nq, gq, SSM_GROUP, SSM_GROUP)
            cols.append(jnp.transpose(kd, (1, 0, 4, 2, 3)))
        rows.append(jnp.stack(cols, axis=3))
    tab_t = jnp.stack(rows, axis=2).reshape(nq, N_PAIR, 2, SSM_GROUP, 2 * LANES)

    vin = _cmul((pw[0][:lc][::-1][..., None], pw[1][:lc][::-1][..., None]),
                (b_bar[0][None], b_bar[1][None]))
    vin = jnp.stack(vin, axis=2).reshape(N_PAIR, 2, nq, gq, 2, SSM_STATE, SSM_GROUP)
    tab_in = jnp.transpose(vin, (2, 0, 1, 6, 3, 4, 5)).reshape(nq, N_PAIR, 2, SSM_GROUP, STATE_COLS)

    cw = _cmul((c[0][None], c[1][None]), (pw[0][1:lc + 1][:, :, None, :], pw[1][1:lc + 1][:, :, None, :]))
    cw = jnp.stack([cw[0], -cw[1]], axis=0).reshape(2, N_PAIR, 2, nq, gq, SSM_GROUP, SSM_STATE)
    tab_out = jnp.transpose(cw, (3, 1, 0, 6, 2, 4, 5)).reshape(nq, N_PAIR, 2 * SSM_STATE, 2 * LANES)

    ar = pw[0][lc].reshape(nq, gq, 1, SSM_STATE)
    ai = pw[1][lc].reshape(nq, gq, 1, SSM_STATE)
    a_same = jnp.concatenate([ar, ar], axis=2).reshape(nq, 1, STATE_COLS)
    a_swap = jnp.concatenate([-ai, ai], axis=2).reshape(nq, 1, STATE_COLS)
    return tab_t, tab_in, tab_out, jnp.concatenate([a_same, a_swap], axis=1)


def _ssm_body(u_ref, tt_ref, tin_ref, tout_ref, a_ref, y_ref, wt_ref, win_ref, wout_ref, st_ref, sw_ref,
              hp_ref, *, bb, nc):
    gq = GROUPS_PER_QBLOCK

    @pl.when(pl.program_id(1) == 0)
    def _():
        grp_t = (lax.broadcasted_iota(jnp.int32, (SSM_GROUP, 2 * LANES), 1) % LANES) // SSM_GROUP
        grp_in = lax.broadcasted_iota(jnp.int32, (SSM_GROUP, STATE_COLS), 1) // LANES
        grp_out = (lax.broadcasted_iota(jnp.int32, (LANES, 2 * LANES), 1) % LANES) // SSM_GROUP
        for d in range(N_PAIR):
            for s2 in range(2):
                piece_t = tt_ref[0, d, s2]
                piece_in = tin_ref[0, d, s2]
                for gl in range(gq):
                    rows = pl.ds(s2 * LANES + gl * SSM_GROUP, SSM_GROUP)
                    wt_ref[d, rows, :] = jnp.where(grp_t == gl, piece_t, 0.0).astype(BF16)
                    win_ref[d, rows, :] = jnp.where(grp_in == gl, piece_in, 0.0).astype(BF16)
            piece_out = tout_ref[0, d]
            for gl in range(gq):
                wout_ref[d, pl.ds(gl * LANES, LANES), :] = jnp.where(grp_out == gl, piece_out, 0.0).astype(BF16)

    xs = []
    for s in range(N_PAIR):
        per_batch = []
        for b in range(bb):
            x0 = u_ref[b, pl.ds(2 * s, nc, stride=SSM_CHUNK), :]
            x1 = u_ref[b, pl.ds(2 * s + 1, nc, stride=SSM_CHUNK), :]
            per_batch.append(jnp.concatenate([x0, x1], axis=1))
        xs.append(jnp.concatenate(per_batch, axis=0).astype(BF16))

    st = _dot(xs[0], win_ref[0])
    for s in range(1, N_PAIR):
        st = st + _dot(xs[s], win_ref[s])
    st_ref[...] = st
    sw_ref[...] = jnp.concatenate(
        [pltpu.roll(st[:, g * LANES:(g + 1) * LANES], SSM_STATE, 1) for g in range(gq)], axis=1)

    a_same = a_ref[0, 0:1, :]
    a_swap = a_ref[0, 1:2, :]

    def step(c, carry):
        new = []
        for b in range(bb):
            h, hs = carry[b]
            row = b * nc + c
            hp_ref[pl.ds(row, 1), :] = h
            new.append((a_same * h + a_swap * hs + st_ref[pl.ds(row, 1), :],
                        a_same * hs - a_swap * h + sw_ref[pl.ds(row, 1), :]))
        return tuple(new)

    zero = jnp.zeros((1, STATE_COLS), F32)
    lax.fori_loop(0, nc, step, tuple((zero, zero) for _ in range(bb)))

    hp = hp_ref[...].astype(BF16)
    for t in range(N_PAIR):
        acc = _dot(hp, wout_ref[t])
        for s in range(t + 1):
            acc = acc + _dot(xs[s], wt_ref[t - s])
        yg = jax.nn.gelu(acc)
        for b in range(bb):
            rows = slice(b * nc, (b + 1) * nc)
            y_ref[b, pl.ds(2 * t, nc, stride=SSM_CHUNK), :] = yg[rows, :LANES]
            y_ref[b, pl.ds(2 * t + 1, nc, stride=SSM_CHUNK), :] = yg[rows, LANES:]


def _ssm(u, tab_t, tab_in, tab_out, a_chunk, bb):
    b, l, _ = u.shape
    nc = l // SSM_CHUNK
    wspec = lambda a: pl.BlockSpec((1,) + a.shape[1:], lambda q, bi: (q,) + (0,) * (a.ndim - 1))
    return pl.pallas_call(
        functools.partial(_ssm_body, bb=bb, nc=nc),
        grid=(SSM_QBLOCKS, b // bb),
        in_specs=[pl.BlockSpec((bb, l, LANES), lambda q, bi: (bi, 0, q)),
                  wspec(tab_t), wspec(tab_in), wspec(tab_out), wspec(a_chunk)],
        out_specs=pl.BlockSpec((bb, l, LANES), lambda q, bi: (bi, 0, q)),
        out_shape=jax.ShapeDtypeStruct((b, l, SSM_WIDTH), F32),
        scratch_shapes=[pltpu.VMEM((N_PAIR, 2 * LANES, 2 * LANES), BF16),
                        pltpu.VMEM((N_PAIR, 2 * LANES, STATE_COLS), BF16),
                        pltpu.VMEM((N_PAIR, STATE_COLS, 2 * LANES), BF16),
                        pltpu.VMEM((bb * nc, STATE_COLS), F32),
                        pltpu.VMEM((bb * nc, STATE_COLS), F32),
                        pltpu.VMEM((bb * nc, STATE_COLS), F32)],
        compiler_params=pltpu.CompilerParams(dimension_semantics=("arbitrary", "arbitrary"),
                                             vmem_limit_bytes=VMEM_LIMIT),
        name="ssm",
    )(u, tab_t, tab_in, tab_out, a_chunk)


def _k4_body(x_ref, o_ref, ys_ref, g_ref, wap_ref, wglu_ref, wout_ref, gffn_ref, wrh_ref, wrl_ref,
             xn_ref, h2_ref, route_ref, routet_ref, cnt_ref, carry_ref, *, tm):
    @pl.when(pl.program_id(0) == 0)
    def _():
        carry_ref[...] = jnp.zeros_like(carry_ref)

    y_attn = _dot(o_ref[...], wap_ref[...])
    vg = _dot(ys_ref[...].astype(BF16), wglu_ref[...])
    y_ssm = vg[:, :D_MODEL] * jax.nn.sigmoid(vg[:, D_MODEL:])
    g = g_ref[...].astype(F32)
    merged = g[:, :D_MODEL] * y_attn + g[:, D_MODEL:] * y_ssm
    xn = x_ref[...] + _dot(merged.astype(BF16), wout_ref[...])
    xn_ref[...] = xn
    h2 = _rms(xn, gffn_ref[...])
    _to_token_tiles(h2_ref, h2, tm)

    hh = h2.astype(BF16)
    hl = (h2 - hh.astype(F32)).astype(BF16)
    logits = _dot(hh, wrh_ref[...]) + _dot(hh, wrl_ref[...]) + _dot(hl, wrh_ref[...])

    lane = lax.broadcasted_iota(jnp.int32, (tm, LANES), 1).astype(F32)
    ninf = jnp.float32(-jnp.inf)
    big = jnp.float32(4 * LANES)

    def top(mask):
        val = jnp.max(jnp.where(mask, logits, ninf), axis=-1, keepdims=True)
        idx = jnp.min(jnp.where(mask & (logits == val), lane, big), axis=-1, keepdims=True)
        return val, idx

    gmask = lane < N_GROUPS
    gmax, gidx = top(gmask)
    g_w = 1.0 / jnp.sum(jnp.where(gmask, jnp.exp(logits - gmax), 0.0), axis=-1, keepdims=True)
    lo = N_GROUPS + PER_GROUP * gidx
    emask = (lane >= lo) & (lane < lo + PER_GROUP)
    l1, i1 = top(emask)
    l2, i2 = top(emask & (lane != i1))
    e21 = jnp.exp(l2 - l1)
    w1 = g_w / (1.0 + e21)
    w2 = g_w * e21 / (1.0 + e21)

    hot1 = lane == i1
    hot2 = lane == i2
    onehot = (hot1 | hot2).astype(BF16)
    tri = (lax.broadcasted_iota(jnp.int32, (tm, tm), 1)
           < lax.broadcasted_iota(jnp.int32, (tm, tm), 0)).astype(BF16)
    before = _dot(tri, onehot) + carry_ref[...]
    r1 = jnp.sum(jnp.where(hot1, before, 0.0), axis=-1, keepdims=True)
    r2 = jnp.sum(jnp.where(hot2, before, 0.0), axis=-1, keepdims=True)
    carry_ref[...] += jnp.sum(onehot.astype(F32), axis=0, keepdims=True)
    cnt_ref[...] = carry_ref[...]

    cols = (w1, w2, i1 - N_GROUPS, i2 - N_GROUPS, r1, r2)
    out = jnp.zeros((tm, LANES), F32)
    for n, col in enumerate(cols):
        out = jnp.where(lane == n, col, out)
    route_ref[...] = out
    routet_ref[...] = out.T[:SUBLANES, :]


def _k4(x, o, ys, gates, wap, wglu, wout, gffn, wrh, wrl, tm):
    t = x.shape[0]
    row = lambda w: pl.BlockSpec((tm, w), lambda i: (i, 0))
    full = lambda a: pl.BlockSpec(a.shape, lambda i: (0,) * a.ndim)
    return pl.pallas_call(
        functools.partial(_k4_body, tm=tm),
        grid=(t // tm,),
        in_specs=[row(D_MODEL), row(HEADS * V_DIM), row(SSM_WIDTH), row(2 * D_MODEL),
                  full(wap), full(wglu), full(wout), full(gffn), full(wrh), full(wrl)],
        out_specs=[row(D_MODEL), pl.BlockSpec((tm * ROW_BLOCKS, LANES), lambda i: (i, 0)), row(LANES),
                   pl.BlockSpec((SUBLANES, tm), lambda i: (0, i)),
                   pl.BlockSpec((1, LANES), lambda i: (0, 0))],
        out_shape=[jax.ShapeDtypeStruct((t, D_MODEL), F32),
                   jax.ShapeDtypeStruct((t * ROW_BLOCKS, LANES), F32),
                   jax.ShapeDtypeStruct((t, LANES), F32), jax.ShapeDtypeStruct((SUBLANES, t), F32),
                   jax.ShapeDtypeStruct((1, LANES), F32)],
        scratch_shapes=[pltpu.VMEM((1, LANES), F32)],
        compiler_params=pltpu.CompilerParams(dimension_semantics=("arbitrary",),
                                             vmem_limit_bytes=VMEM_LIMIT),
        name="k4_merge_router",
    )(x, o, ys, gates, wap, wglu, wout, gffn, wrh, wrl)


def _row_copy(src, dst, sem):
    return pltpu.make_async_copy(src, dst, sem)


def _token_tile(ref, r):
    return ref.at[pl.ds(pl.multiple_of(r * ROW_BLOCKS, ROW_BLOCKS), ROW_BLOCKS)]


def _dispatch_body(zt_ref, nz_ref, pos_ref, h2_ref, xs_ref, zeros_ref, sem, *, td):
    tile_rows = ROW_TILE * ROW_BLOCKS

    @pl.when(pl.program_id(0) == 0)
    def _():
        zeros_ref[...] = jnp.zeros_like(zeros_ref)
        n = nz_ref[0]

        def zstart(z, _):
            _row_copy(zeros_ref, xs_ref.at[pl.ds(pl.multiple_of(zt_ref[z] * tile_rows, tile_rows), tile_rows)],
                      sem).start()
            return 0

        def zwait(z, _):
            _row_copy(zeros_ref, xs_ref.at[pl.ds(0, tile_rows)], sem).wait()
            return 0

        lax.fori_loop(0, n, zstart, 0)
        lax.fori_loop(0, n, zwait, 0)

    def start(r, _):
        for kk in range(2):
            _row_copy(_token_tile(h2_ref, r), _token_tile(xs_ref, pos_ref[0, kk, r]), sem).start(priority=kk)
        return 0

    lax.fori_loop(0, td, start, 0, unroll=ROW_UNROLL)
    for kk in range(2):
        _row_copy(h2_ref, xs_ref.at[pl.ds(0, td * ROW_BLOCKS)], sem).wait()


def _dispatch(zero_tiles, n_zero, pos, h2, n_rows, td):
    t = h2.shape[0] // ROW_BLOCKS
    grid_spec = pltpu.PrefetchScalarGridSpec(
        num_scalar_prefetch=2,
        grid=(t // td,),
        in_specs=[pl.BlockSpec((1, 2, td), lambda i, zt, nz: (i, 0, 0), memory_space=pltpu.SMEM),
                  pl.BlockSpec((td * ROW_BLOCKS, LANES), lambda i, zt, nz: (i, 0))],
        out_specs=pl.BlockSpec(memory_space=pl.ANY),
        scratch_shapes=[pltpu.VMEM((ROW_TILE * ROW_BLOCKS, LANES), F32), pltpu.SemaphoreType.DMA(())],
    )
    return pl.pallas_call(
        functools.partial(_dispatch_body, td=td),
        grid_spec=grid_spec,
        out_shape=jax.ShapeDtypeStruct((n_rows * ROW_BLOCKS, LANES), F32),
        compiler_params=pltpu.CompilerParams(dimension_semantics=("arbitrary",)),
        name="moe_dispatch",
    )(zero_tiles, n_zero, pos, h2)


def _expert_body(te_ref, tv_ref, xi_ref, xs_ref, wg_ref, wu_ref, wd_ref, ys_ref, wgb_ref, wub_ref, wdb_ref):
    i = pl.program_id(0)
    prev = te_ref[jnp.maximum(i - 1, 0)]

    @pl.when((i == 0) | (te_ref[i] != prev))
    def _():
        wgb_ref[...] = wg_ref[0].astype(BF16)
        wub_ref[...] = wu_ref[0].astype(BF16)
        wdb_ref[...] = wd_ref[0].astype(BF16)

    @pl.when(tv_ref[i] == 1)
    def _():
        xb = _from_token_tiles(xs_ref, ROW_TILE).astype(BF16)
        hidden = jax.nn.silu(_dot(xb, wgb_ref[...])) * _dot(xb, wub_ref[...])
        _to_token_tiles(ys_ref, _dot(hidden.astype(BF16), wdb_ref[...]), ROW_TILE)

    @pl.when(tv_ref[i] == 0)
    def _():
        ys_ref[...] = jnp.zeros_like(ys_ref)


def _experts(tile_expert, tile_valid, tile_src, xs, wg, wu, wd):
    tile_rows = ROW_TILE * ROW_BLOCKS
    grid_spec = pltpu.PrefetchScalarGridSpec(
        num_scalar_prefetch=3,
        grid=(xs.shape[0] // tile_rows,),
        in_specs=[pl.BlockSpec((tile_rows, LANES), lambda i, te, tv, xi: (xi[i], 0)),
                  pl.BlockSpec((1, D_MODEL, D_EXPERT), lambda i, te, tv, xi: (te[i], 0, 0)),
                  pl.BlockSpec((1, D_MODEL, D_EXPERT), lambda i, te, tv, xi: (te[i], 0, 0)),
                  pl.BlockSpec((1, D_EXPERT, D_MODEL), lambda i, te, tv, xi: (te[i], 0, 0))],
        out_specs=pl.BlockSpec((tile_rows, LANES), lambda i, te, tv, xi: (i, 0)),
        scratch_shapes=[pltpu.VMEM((D_MODEL, D_EXPERT), BF16), pltpu.VMEM((D_MODEL, D_EXPERT), BF16),
                        pltpu.VMEM((D_EXPERT, D_MODEL), BF16)],
    )
    return pl.pallas_call(
        _expert_body,
        grid_spec=grid_spec,
        out_shape=jax.ShapeDtypeStruct(xs.shape, F32),
        compiler_params=pltpu.CompilerParams(dimension_semantics=("arbitrary",),
                                             vmem_limit_bytes=VMEM_LIMIT),
        name="moe_experts",
    )(tile_expert, tile_valid, tile_src, xs, wg, wu, wd)


def _combine_body(pos_ref, x_ref, route_ref, ys_ref, out_ref, buf_ref, sem, *, tc):
    def start(r, _):
        for kk in range(2):
            _row_copy(_token_tile(ys_ref, pos_ref[0, kk, r]), _token_tile(buf_ref.at[kk], r),
                      sem).start(priority=kk)
        return 0

    lax.fori_loop(0, tc, start, 0, unroll=ROW_UNROLL)
    for kk in range(2):
        _row_copy(ys_ref.at[pl.ds(0, tc * ROW_BLOCKS)], buf_ref.at[kk], sem).wait()
    route = route_ref[...]
    out_ref[...] = (x_ref[...] + route[:, 0:1] * _from_token_tiles(buf_ref.at[0], tc)
                    + route[:, 1:2] * _from_token_tiles(buf_ref.at[1], tc))


def _combine(pos, x, route, ys, tc):
    t = x.shape[0]
    return pl.pallas_call(
        functools.partial(_combine_body, tc=tc),
        grid=(t // tc,),
        in_specs=[pl.BlockSpec((1, 2, tc), lambda i: (i, 0, 0), memory_space=pltpu.SMEM),
                  pl.BlockSpec((tc, D_MODEL), lambda i: (i, 0)),
                  pl.BlockSpec((tc, LANES), lambda i: (i, 0)),
                  pl.BlockSpec(memory_space=pl.ANY)],
        out_specs=pl.BlockSpec((tc, D_MODEL), lambda i: (i, 0)),
        out_shape=jax.ShapeDtypeStruct((t, D_MODEL), F32),
        scratch_shapes=[pltpu.VMEM((2, tc * ROW_BLOCKS, LANES), F32), pltpu.SemaphoreType.DMA(())],
        compiler_params=pltpu.CompilerParams(dimension_semantics=("arbitrary",)),
        name="moe_combine",
    )(pos, x, route, ys)


def _routing_plan(route_t, counts, n_tiles, td):
    t = route_t.shape[1]
    cnt = counts[0, N_GROUPS:N_GROUPS + N_EXPERTS].astype(jnp.int32)
    padded = ((cnt + ROW_TILE - 1) // ROW_TILE) * ROW_TILE
    ends = jnp.cumsum(padded)
    offsets = ends - padded
    experts = jnp.arange(N_EXPERTS, dtype=jnp.int32)[:, None]

    def position(e_row, r_row):
        e = e_row.astype(jnp.int32)[None, :]
        return jnp.sum(jnp.where(e == experts, offsets[:, None], 0), axis=0) + r_row.astype(jnp.int32)

    pos = jnp.stack([position(route_t[2], route_t[4]).reshape(t // td, td),
                     position(route_t[3], route_t[5]).reshape(t // td, td)], axis=1)

    used = ends[-1] // ROW_TILE
    tiles = jnp.arange(n_tiles, dtype=jnp.int32)
    valid = tiles < used
    last = jnp.maximum(used - 1, 0)
    expert_of = jnp.sum((ends[None, :] <= (tiles * ROW_TILE)[:, None]).astype(jnp.int32), axis=1)
    tile_expert = jnp.where(valid, expert_of, jnp.sum(jnp.where(tiles == last, expert_of, 0)))
    tile_src = jnp.minimum(tiles, last)
    last_tile = ends // ROW_TILE - 1
    has_pad = (cnt > 0) & (padded > cnt)
    zero_tiles = jnp.concatenate([jnp.where(has_pad, last_tile, -1).astype(jnp.int32),
                                  jnp.where(valid, -1, tiles)])
    order = jnp.argsort(zero_tiles < 0, stable=True)
    zero_tiles = zero_tiles[order]
    n_zero = jnp.sum(zero_tiles >= 0).astype(jnp.int32).reshape(1)
    return pos, tile_expert, valid.astype(jnp.int32), tile_src, jnp.maximum(zero_tiles, 0), n_zero


def _rope_tables(positions):
    half = ROPE // 2
    inv_freq = ROPE_THETA ** (-jnp.arange(half, dtype=F32) / half)
    ang = positions.astype(F32).reshape(-1, 1) * jnp.tile(inv_freq, LANES // half)[None, :]
    return jnp.cos(ang), jnp.sin(ang)


def _head_slots(w, width):
    k = w.shape[0]
    w = w.reshape(k, HEADS, width)
    return jnp.pad(w, ((0, 0), (0, 0), (0, LANES - width))).reshape(k, HEADS * LANES)


def _layer_weights(l, w_in, w_uq, w_ukv, q_head_g, k_head_g, w_router_group, w_router_expert):
    wi = w_in[l]
    kr = jnp.pad(wi[:, C_KR:C_KR + ROPE], ((0, 0), (NOPE, LANES - QK_DIM)))
    win = jnp.concatenate([wi[:, :C_KR], kr, wi[:, C_KR + ROPE:]], axis=1).astype(BF16)
    wuq = _head_slots(w_uq[l], QK_DIM).astype(BF16)
    kv = w_ukv[l].reshape(KV_LORA, HEADS, NOPE + V_DIM)
    wuk = _head_slots(kv[:, :, :NOPE].reshape(KV_LORA, HEADS * NOPE), NOPE)
    wuv = _head_slots(kv[:, :, NOPE:].reshape(KV_LORA, HEADS * V_DIM), V_DIM)
    wukv = jnp.concatenate([wuk, wuv], axis=1).astype(BF16)
    pad_g = lambda g: jnp.tile(jnp.pad(g.astype(F32), (0, LANES - QK_DIM)), HEADS)[None, :]
    gq = pad_g(q_head_g[l]) * (QK_DIM ** -0.5 * math.log2(math.e))
    gk = pad_g(k_head_g[l])
    wr = jnp.concatenate([w_router_group[l], w_router_expert[l]], axis=1).astype(F32)
    wr = jnp.pad(wr, ((0, 0), (0, LANES - wr.shape[1])))
    wrh = wr.astype(BF16)
    wrl = (wr - wrh.astype(F32)).astype(BF16)
    return win, wuq, wukv, gq, gk, wrh, wrl


def kernel(x, positions, norm_mix_g, w_in, q_lora_g, w_uq, kv_lora_g, w_ukv, q_head_g, k_head_g, w_attn_proj, lam_re, lam_im, log_dt, b_re, b_im, c_re, c_im, d_skip, w_glu, w_out, norm_ffn_g, w_router_group, w_router_expert, w_exp_gate, w_exp_up, w_exp_down):
    bsz, seq, _ = x.shape
    t = bsz * seq
    depth = w_in.shape[0]
    tm = 512 if t % 512 == 0 else 256
    tq = 256
    bb = 2 if bsz % 2 == 0 else 1
    n_tiles = (2 * t) // ROW_TILE + N_EXPERTS
    n_rows = n_tiles * ROW_TILE
    row1 = lambda g: g.astype(F32)[None, :]

    cos_t, sin_t = _rope_tables(positions)
    xf = x.reshape(t, D_MODEL).astype(F32)
    for l in range(depth):
        win, wuq, wukv, gq, gk, wrh, wrl = _layer_weights(
            l, w_in, w_uq, w_ukv, q_head_g, k_head_g, w_router_group, w_router_expert)
        q, k, v, u, gates = _k1(xf, row1(norm_mix_g[l]), win, row1(q_lora_g[l]), wuq,
                                row1(kv_lora_g[l]), wukv, gq, gk, cos_t, sin_t, tm)
        o = _attention(q.reshape(bsz, seq, -1), k.reshape(bsz, seq, -1), v.reshape(bsz, seq, -1), tq)
        tables = _ssm_tables(lam_re[l], lam_im[l], log_dt[l], b_re[l], b_im[l], c_re[l], c_im[l], d_skip[l])
        ys = _ssm(u.reshape(bsz, seq, SSM_WIDTH), *tables, bb)
        xn, h2, route, route_t, counts = _k4(xf, o.reshape(t, -1), ys.reshape(t, SSM_WIDTH), gates,
                                             w_attn_proj[l].astype(BF16), w_glu[l].astype(BF16),
                                             w_out[l].astype(BF16), row1(norm_ffn_g[l]), wrh, wrl, tm)
        pos, tile_expert, tile_valid, tile_src, zero_tiles, n_zero = _routing_plan(route_t, counts, n_tiles, tm)
        xs = _dispatch(zero_tiles, n_zero, pos, h2, n_rows, tm)
        ye = _experts(tile_expert, tile_valid, tile_src, xs, w_exp_gate[l], w_exp_up[l], w_exp_down[l])
        xf = _combine(pos, xn, route, ye, tm)
    return xf.reshape(bsz, seq, D_MODEL).astype(x.dtype)
```

```python
import functools
import math

import jax
import jax.numpy as jnp
from jax import lax
from jax.experimental import pallas as pl
from jax.experimental.pallas import tpu as pltpu

F32 = jnp.float32
BF16 = jnp.bfloat16

D_MODEL = 1024
CHUNK = 64
HEADS = 8
NOPE = 64
ROPE = 32
QK_DIM = NOPE + ROPE
V_DIM = 64
Q_LORA = 384
KV_LORA = 256
ROPE_THETA = 10000.0
SSM_WIDTH = 512
SSM_GROUP = 16
SSM_GROUPS = 32
SSM_STATE = 64
N_GROUPS = 4
PER_GROUP = 8
N_EXPERTS = 32
D_EXPERT = 256
EPS = 1e-6

LANES = 128
SUBLANES = 8
SSM_CHUNK = 16
N_PAIR = SSM_CHUNK // 2
SSM_QBLOCKS = SSM_WIDTH // LANES
GROUPS_PER_QBLOCK = LANES // SSM_GROUP
STATE_COLS = GROUPS_PER_QBLOCK * 2 * SSM_STATE
ROW_TILE = 512
ROW_UNROLL = 8
K1_SLABS = 1
K4_SLABS = 2
ROW_BLOCKS = D_MODEL // LANES
VMEM_LIMIT = 56 * 1024 * 1024

C_Q = 0
C_KV = C_Q + Q_LORA
C_KR = C_KV + KV_LORA
C_U = C_KR + LANES
C_GATE = C_U + SSM_WIDTH
IN_COLS_PAD = C_GATE + 2 * D_MODEL


def _rms(x, g):
    return x * lax.rsqrt(jnp.mean(x * x, axis=-1, keepdims=True) + EPS) * g


def _dot(a, b):
    return jnp.dot(a, b, preferred_element_type=F32)


def _to_token_tiles(ref, x, n, base=0):
    for k in range(ROW_BLOCKS):
        ref[pl.ds(base * ROW_BLOCKS + k, n, stride=ROW_BLOCKS), :] = x[:, k * LANES:(k + 1) * LANES]


def _from_token_tiles(ref, n):
    return jnp.concatenate([ref[pl.ds(k, n, stride=ROW_BLOCKS), :] for k in range(ROW_BLOCKS)], axis=1)


def _k1_body(x_ref, gmix_ref, win_ref, gql_ref, wuq_ref, gkvl_ref, wukv_ref, gq_ref, gk_ref,
             cos_ref, sin_ref, q_ref, k_ref, v_ref, u_ref, gates_ref):
    vlane = lax.broadcasted_iota(jnp.int32, (1, HEADS * LANES), 1)
    ones_col = ((vlane % LANES) == V_DIM).astype(F32)
    ts = x_ref.shape[0] // K1_SLABS
    for s in range(K1_SLABS):
        rows = pl.ds(s * ts, ts)
        hb = _rms(x_ref[rows, :], gmix_ref[...]).astype(BF16)

        def proj(lo, hi):
            return _dot(hb, win_ref[:, lo:hi])

        lane = lax.broadcasted_iota(jnp.int32, (ts, LANES), 1)
        sin = sin_ref[rows, :]
        cos_t = jnp.where(lane < NOPE, 1.0, jnp.where(lane < QK_DIM, cos_ref[rows, :], 0.0))
        sin_m = jnp.where((lane >= NOPE) & (lane < NOPE + ROPE // 2), -sin, 0.0)
        sin_p = jnp.where((lane >= NOPE + ROPE // 2) & (lane < QK_DIM), sin, 0.0)

        def head_norm_rope(r, g):
            ms = jnp.sum(r * r, axis=-1, keepdims=True) * (1.0 / QK_DIM)
            rn = r * lax.rsqrt(ms + EPS) * g
            return (rn * cos_t + pltpu.roll(rn, LANES - ROPE // 2, 1) * sin_m
                    + pltpu.roll(rn, ROPE // 2, 1) * sin_p)

        qn = _rms(proj(C_Q, C_KV), gql_ref[...]).astype(BF16)
        qf = _dot(qn, wuq_ref[...])
        for h in range(HEADS):
            sl = slice(h * LANES, (h + 1) * LANES)
            q_ref[rows, sl] = head_norm_rope(qf[:, sl], gq_ref[:, sl]).astype(BF16)

        kvn = _rms(proj(C_KV, C_KR), gkvl_ref[...]).astype(BF16)
        kf = _dot(kvn, wukv_ref[...])
        kr = proj(C_KR, C_U)
        for h in range(HEADS):
            sl = slice(h * LANES, (h + 1) * LANES)
            k_ref[rows, sl] = head_norm_rope(kf[:, sl] + kr, gk_ref[:, sl]).astype(BF16)
        v_ref[rows, :] = (kf[:, HEADS * LANES:] + ones_col).astype(BF16)

        u_ref[rows, :] = proj(C_U, C_GATE)
        gates_ref[rows, :] = jax.nn.sigmoid(proj(C_GATE, IN_COLS_PAD)).astype(BF16)


def _k1(x, gmix, win, gql, wuq, gkvl, wukv, gq, gk, cos_t, sin_t, tm):
    t = x.shape[0]
    row = lambda w: pl.BlockSpec((tm, w), lambda i: (i, 0))
    full = lambda a: pl.BlockSpec(a.shape, lambda i: (0,) * a.ndim)
    return pl.pallas_call(
        _k1_body,
        grid=(t // tm,),
        in_specs=[row(D_MODEL), full(gmix), full(win), full(gql), full(wuq), full(gkvl), full(wukv),
                  full(gq), full(gk), row(LANES), row(LANES)],
        out_specs=[row(HEADS * LANES), row(HEADS * LANES), row(HEADS * LANES), row(SSM_WIDTH),
                   row(2 * D_MODEL)],
        out_shape=[jax.ShapeDtypeStruct((t, HEADS * LANES), BF16),
                   jax.ShapeDtypeStruct((t, HEADS * LANES), BF16),
                   jax.ShapeDtypeStruct((t, HEADS * LANES), BF16),
                   jax.ShapeDtypeStruct((t, SSM_WIDTH), F32),
                   jax.ShapeDtypeStruct((t, 2 * D_MODEL), BF16)],
        compiler_params=pltpu.CompilerParams(dimension_semantics=("arbitrary",),
                                             vmem_limit_bytes=VMEM_LIMIT),
        name="k1_inproj",
    )(x, gmix, win, gql, wuq, gkvl, wukv, gq, gk, cos_t, sin_t)


def _attn_body(q_ref, k_ref, v_ref, o_ref, *, tq, nq):
    i = pl.program_id(1)
    row_chunk = lax.broadcasted_iota(jnp.int32, (tq, tq), 0) // CHUNK
    col_chunk = lax.broadcasted_iota(jnp.int32, (tq, tq), 1) // CHUNK
    diag_mask = col_chunk <= row_chunk
    neg = jnp.float32(-1e30)

    def scores(hh, n_full):
        sl = slice(hh * LANES, (hh + 1) * LANES)
        s = lax.dot_general(q_ref[0, :, sl], k_ref[0, :n_full + tq, sl], (((1,), (1,)), ((), ())),
                            preferred_element_type=F32)
        s_diag = jnp.where(diag_mask, s[:, n_full:], neg)
        return jnp.concatenate([s[:, :n_full], s_diag], axis=1) if n_full else s_diag

    def probs(s):
        return jnp.exp2(s - jnp.max(s, axis=-1, keepdims=True)).astype(BF16)

    def values(p, hh, n_full):
        of = _dot(p, v_ref[0, :n_full + tq, hh * LANES:(hh + 1) * LANES])
        return of * (1.0 / of[:, V_DIM:V_DIM + 1])

    for ii in range(nq):
        @pl.when(i == ii)
        def _(ii=ii):
            n_full = ii * tq
            lane = lax.broadcasted_iota(jnp.int32, (tq, LANES), 1)
            ss = [scores(hh, n_full) for hh in range(HEADS)]
            ps = [probs(s) for s in ss]
            os_ = [values(p, hh, n_full) for hh, p in enumerate(ps)]
            for hp in range(HEADS // 2):
                o_ref[0, :, hp * LANES:(hp + 1) * LANES] = jnp.where(
                    lane < V_DIM, os_[2 * hp], pltpu.roll(os_[2 * hp + 1], V_DIM, 1)).astype(BF16)


def _attention(q, k, v, tq):
    b, l, _ = q.shape
    nq = l // tq
    return pl.pallas_call(
        functools.partial(_attn_body, tq=tq, nq=nq),
        grid=(b, nq),
        in_specs=[pl.BlockSpec((1, tq, HEADS * LANES), lambda bi, i: (bi, i, 0)),
                  pl.BlockSpec((1, l, HEADS * LANES), lambda bi, i: (bi, 0, 0)),
                  pl.BlockSpec((1, l, HEADS * LANES), lambda bi, i: (bi, 0, 0))],
        out_specs=pl.BlockSpec((1, tq, HEADS * V_DIM), lambda bi, i: (bi, i, 0)),
        out_shape=jax.ShapeDtypeStruct((b, l, HEADS * V_DIM), BF16),
        compiler_params=pltpu.CompilerParams(
            dimension_semantics=("arbitrary", "arbitrary"), vmem_limit_bytes=VMEM_LIMIT),
        name="attention",
    )(q, k, v)


def _cmul(a, b):
    return a[0] * b[0] - a[1] * b[1], a[0] * b[1] + a[1] * b[0]


def _ssm_tables(lam_re, lam_im, log_dt, b_re, b_im, c_re, c_im, d_skip):
    lc = SSM_CHUNK
    nq, gq = SSM_QBLOCKS, GROUPS_PER_QBLOCK
    lr, li = lam_re.astype(F32), lam_im.astype(F32)
    dt = jnp.exp(log_dt.astype(F32))[:, None]
    steps = jnp.arange(lc + 1, dtype=F32)[:, None, None]
    mag = jnp.exp(lr * dt * steps)
    pw = (mag * jnp.cos(li * dt * steps), mag * jnp.sin(li * dt * steps))
    num = (pw[0][1] - 1.0, pw[1][1])
    den = lr * lr + li * li
    ratio = ((num[0] * lr + num[1] * li) / den, (num[1] * lr - num[0] * li) / den)
    b_bar = _cmul((ratio[0][..., None], ratio[1][..., None]), (b_re.astype(F32), b_im.astype(F32)))
    c = (c_re.astype(F32), c_im.astype(F32))

    m = _cmul((pw[0][:lc, :, :, None], pw[1][:lc, :, :, None]), (b_bar[0][None], b_bar[1][None]))
    kern = jnp.einsum('gip,dgpj->dgij', c[0], m[0]) - jnp.einsum('gip,dgpj->dgij', c[1], m[1])
    kern = kern.at[0].add(jnp.eye(SSM_GROUP, dtype=F32)[None] * d_skip.astype(F32)[:, :, None])
    kpad = jnp.concatenate([jnp.zeros_like(kern[:1]), kern], axis=0)
    big_d = jnp.arange(N_PAIR)
    rows = []
    for s2 in range(2):
        cols = []
        for t2 in range(2):
            kd = kpad[2 * big_d + t2 - s2 + 1].reshape(N_PAIR, nq, gq, SSM_GROUP, SSM_GROUP)
            cols.append(jnp.transpose(kd, (1, 0, 4, 2, 3)))
        rows.append(jnp.stack(cols, axis=3))
    tab_t = jnp.stack(rows, axis=2).reshape(nq, N_PAIR, 2, SSM_GROUP, 2 * LANES)

    bt = (jnp.swapaxes(b_bar[0], 1, 2), jnp.swapaxes(b_bar[1], 1, 2))
    vin = _cmul((pw[0][:lc][::-1][:, :, None, :], pw[1][:lc][::-1][:, :, None, :]), (bt[0][None], bt[1][None]))
    vin = jnp.concatenate(vin, axis=-1).reshape(N_PAIR, 2, nq, gq, SSM_GROUP, 2 * SSM_STATE)
    tab_in = jnp.transpose(vin, (2, 0, 1, 4, 3, 5)).reshape(nq, N_PAIR, 2, SSM_GROUP, STATE_COLS)

    cw = _cmul((c[0][None], c[1][None]), (pw[0][1:lc + 1][:, :, None, :], pw[1][1:lc + 1][:, :, None, :]))
    cw = jnp.concatenate([cw[0], -cw[1]], axis=-1).reshape(N_PAIR, 2, nq, gq, SSM_GROUP, 2 * SSM_STATE)
    tab_out = jnp.transpose(cw, (2, 0, 1, 3, 4, 5)).reshape(nq, N_PAIR, 2 * LANES, 2 * SSM_STATE)

    ar = pw[0][lc].reshape(nq, gq, 1, SSM_STATE)
    ai = pw[1][lc].reshape(nq, gq, 1, SSM_STATE)
    a_same = jnp.concatenate([ar, ar], axis=2).reshape(nq, 1, STATE_COLS)
    a_swap = jnp.concatenate([-ai, ai], axis=2).reshape(nq, 1, STATE_COLS)
    return tab_t, tab_in, tab_out, jnp.concatenate([a_same, a_swap], axis=1)


def _ssm_body(u_ref, tt_ref, tin_ref, tout_ref, a_ref, y_ref, wt_ref, win_ref, wout_ref, st_ref, sw_ref,
              hp_ref, *, bb, nc):
    gq = GROUPS_PER_QBLOCK

    @pl.when(pl.program_id(1) == 0)
    def _():
        grp_t = (lax.broadcasted_iota(jnp.int32, (SSM_GROUP, 2 * LANES), 1) % LANES) // SSM_GROUP
        grp_in = lax.broadcasted_iota(jnp.int32, (SSM_GROUP, STATE_COLS), 1) // LANES
        grp_out = (lax.broadcasted_iota(jnp.int32, (LANES, 2 * LANES), 1) % LANES) // SSM_GROUP
        for d in range(N_PAIR):
            for s2 in range(2):
                piece_t = tt_ref[0, d, s2]
                piece_in = tin_ref[0, d, s2]
                for gl in range(gq):
                    rows = pl.ds(s2 * LANES + gl * SSM_GROUP, SSM_GROUP)
                    wt_ref[d, rows, :] = jnp.where(grp_t == gl, piece_t, 0.0).astype(BF16)
                    win_ref[d, rows, :] = jnp.where(grp_in == gl, piece_in, 0.0).astype(BF16)
            piece_out = tout_ref[0, d].T
            for gl in range(gq):
                wout_ref[d, pl.ds(gl * LANES, LANES), :] = jnp.where(grp_out == gl, piece_out, 0.0).astype(BF16)

    xs = []
    for s in range(N_PAIR):
        per_batch = []
        for b in range(bb):
            x0 = u_ref[b, pl.ds(2 * s, nc, stride=SSM_CHUNK), :]
            x1 = u_ref[b, pl.ds(2 * s + 1, nc, stride=SSM_CHUNK), :]
            per_batch.append(jnp.concatenate([x0, x1], axis=1))
        xs.append(jnp.concatenate(per_batch, axis=0).astype(BF16))

    st = _dot(xs[0], win_ref[0])
    for s in range(1, N_PAIR):
        st = st + _dot(xs[s], win_ref[s])
    st_ref[...] = st
    sw_ref[...] = jnp.concatenate(
        [pltpu.roll(st[:, g * LANES:(g + 1) * LANES], SSM_STATE, 1) for g in range(gq)], axis=1)

    a_same = a_ref[0, 0:1, :]
    a_swap = a_ref[0, 1:2, :]

    def step(c, carry):
        new = []
        for b in range(bb):
            h, hs = carry[b]
            row = b * nc + c
            hp_ref[pl.ds(row, 1), :] = h
            new.append((a_same * h + a_swap * hs + st_ref[pl.ds(row, 1), :],
                        a_same * hs - a_swap * h + sw_ref[pl.ds(row, 1), :]))
        return tuple(new)

    zero = jnp.zeros((1, STATE_COLS), F32)
    lax.fori_loop(0, nc, step, tuple((zero, zero) for _ in range(bb)))

    hp = hp_ref[...].astype(BF16)
    for t in range(N_PAIR):
        acc = _dot(hp, wout_ref[t])
        for s in range(t + 1):
            acc = acc + _dot(xs[s], wt_ref[t - s])
        yg = jax.nn.gelu(acc)
        for b in range(bb):
            rows = slice(b * nc, (b + 1) * nc)
            y_ref[b, pl.ds(2 * t, nc, stride=SSM_CHUNK), :] = yg[rows, :LANES]
            y_ref[b, pl.ds(2 * t + 1, nc, stride=SSM_CHUNK), :] = yg[rows, LANES:]


def _ssm(u, tab_t, tab_in, tab_out, a_chunk, bb):
    b, l, _ = u.shape
    nc = l // SSM_CHUNK
    wspec = lambda a: pl.BlockSpec((1,) + a.shape[1:], lambda q, bi: (q,) + (0,) * (a.ndim - 1))
    return pl.pallas_call(
        functools.partial(_ssm_body, bb=bb, nc=nc),
        grid=(SSM_QBLOCKS, b // bb),
        in_specs=[pl.BlockSpec((bb, l, LANES), lambda q, bi: (bi, 0, q)),
                  wspec(tab_t), wspec(tab_in), wspec(tab_out), wspec(a_chunk)],
        out_specs=pl.BlockSpec((bb, l, LANES), lambda q, bi: (bi, 0, q)),
        out_shape=jax.ShapeDtypeStruct((b, l, SSM_WIDTH), F32),
        scratch_shapes=[pltpu.VMEM((N_PAIR, 2 * LANES, 2 * LANES), BF16),
                        pltpu.VMEM((N_PAIR, 2 * LANES, STATE_COLS), BF16),
                        pltpu.VMEM((N_PAIR, STATE_COLS, 2 * LANES), BF16),
                        pltpu.VMEM((bb * nc, STATE_COLS), F32),
                        pltpu.VMEM((bb * nc, STATE_COLS), F32),
                        pltpu.VMEM((bb * nc, STATE_COLS), F32)],
        compiler_params=pltpu.CompilerParams(dimension_semantics=("arbitrary", "arbitrary"),
                                             vmem_limit_bytes=VMEM_LIMIT),
        name="ssm",
    )(u, tab_t, tab_in, tab_out, a_chunk)


def _k4_body(x_ref, o_ref, ys_ref, g_ref, wap_ref, wglu_ref, wout_ref, gffn_ref, wrh_ref, wrl_ref,
             xn_ref, h2_ref, route_ref, routet_ref, cnt_ref, carry_ref, *, tm):
    @pl.when(pl.program_id(0) == 0)
    def _():
        carry_ref[...] = jnp.zeros_like(carry_ref)

    ts = tm // K4_SLABS
    slabs = [pl.ds(s * ts, ts) for s in range(K4_SLABS)]

    logits_all = []
    for s, rows in enumerate(slabs):
        y_attn = _dot(o_ref[rows, :], wap_ref[...])
        vg = _dot(ys_ref[rows, :].astype(BF16), wglu_ref[...])
        y_ssm = vg[:, :D_MODEL] * jax.nn.sigmoid(vg[:, D_MODEL:])
        g = g_ref[rows, :].astype(F32)
        merged = g[:, :D_MODEL] * y_attn + g[:, D_MODEL:] * y_ssm
        xn = x_ref[rows, :] + _dot(merged.astype(BF16), wout_ref[...])
        xn_ref[rows, :] = xn
        h2 = _rms(xn, gffn_ref[...])
        _to_token_tiles(h2_ref, h2, ts, base=s * ts)
        hh = h2.astype(BF16)
        hl = (h2 - hh.astype(F32)).astype(BF16)
        logits_all.append(_dot(hh, wrh_ref[...]) + _dot(hh, wrl_ref[...]) + _dot(hl, wrh_ref[...]))

    lane = lax.broadcasted_iota(jnp.int32, (ts, LANES), 1).astype(F32)
    ninf = jnp.float32(-jnp.inf)
    big = jnp.float32(4 * LANES)
    tri = (lax.broadcasted_iota(jnp.int32, (ts, ts), 1)
           < lax.broadcasted_iota(jnp.int32, (ts, ts), 0)).astype(BF16)
    seen = carry_ref[...]
    for s, rows in enumerate(slabs):
        logits = logits_all[s]

        def top(mask):
            val = jnp.max(jnp.where(mask, logits, ninf), axis=-1, keepdims=True)
            idx = jnp.min(jnp.where(mask & (logits == val), lane, big), axis=-1, keepdims=True)
            return val, idx

        gmask = lane < N_GROUPS
        gmax, gidx = top(gmask)
        g_w = 1.0 / jnp.sum(jnp.where(gmask, jnp.exp(logits - gmax), 0.0), axis=-1, keepdims=True)
        lo = N_GROUPS + PER_GROUP * gidx
        emask = (lane >= lo) & (lane < lo + PER_GROUP)
        l1, i1 = top(emask)
        l2, i2 = top(emask & (lane != i1))
        e21 = jnp.exp(l2 - l1)
        w1 = g_w / (1.0 + e21)
        w2 = g_w * e21 / (1.0 + e21)

        hot1 = lane == i1
        hot2 = lane == i2
        onehot = (hot1 | hot2).astype(BF16)
        before = _dot(tri, onehot) + seen
        r1 = jnp.sum(jnp.where(hot1, before, 0.0), axis=-1, keepdims=True)
        r2 = jnp.sum(jnp.where(hot2, before, 0.0), axis=-1, keepdims=True)
        seen = seen + jnp.sum(onehot.astype(F32), axis=0, keepdims=True)

        cols = (w1, w2, i1 - N_GROUPS, i2 - N_GROUPS, r1, r2)
        out = jnp.zeros((ts, LANES), F32)
        for n, col in enumerate(cols):
            out = jnp.where(lane == n, col, out)
        route_ref[rows, :] = out
        routet_ref[:, rows] = out.T[:SUBLANES, :]
    carry_ref[...] = seen
    cnt_ref[...] = seen


def _k4(x, o, ys, gates, wap, wglu, wout, gffn, wrh, wrl, tm):
    t = x.shape[0]
    row = lambda w: pl.BlockSpec((tm, w), lambda i: (i, 0))
    full = lambda a: pl.BlockSpec(a.shape, lambda i: (0,) * a.ndim)
    return pl.pallas_call(
        functools.partial(_k4_body, tm=tm),
        grid=(t // tm,),
        in_specs=[row(D_MODEL), row(HEADS * V_DIM), row(SSM_WIDTH), row(2 * D_MODEL),
                  full(wap), full(wglu), full(wout), full(gffn), full(wrh), full(wrl)],
        out_specs=[row(D_MODEL), pl.BlockSpec((tm * ROW_BLOCKS, LANES), lambda i: (i, 0)), row(LANES),
                   pl.BlockSpec((SUBLANES, tm), lambda i: (0, i)),
                   pl.BlockSpec((1, LANES), lambda i: (0, 0))],
        out_shape=[jax.ShapeDtypeStruct((t, D_MODEL), F32),
                   jax.ShapeDtypeStruct((t * ROW_BLOCKS, LANES), F32),
                   jax.ShapeDtypeStruct((t, LANES), F32), jax.ShapeDtypeStruct((SUBLANES, t), F32),
                   jax.ShapeDtypeStruct((1, LANES), F32)],
        scratch_shapes=[pltpu.VMEM((1, LANES), F32)],
        compiler_params=pltpu.CompilerParams(dimension_semantics=("arbitrary",),
                                             vmem_limit_bytes=VMEM_LIMIT),
        name="k4_merge_router",
    )(x, o, ys, gates, wap, wglu, wout, gffn, wrh, wrl)


def _row_copy(src, dst, sem):
    return pltpu.make_async_copy(src, dst, sem)


def _token_tile(ref, r):
    return ref.at[pl.ds(pl.multiple_of(r * ROW_BLOCKS, ROW_BLOCKS), ROW_BLOCKS)]


def _dispatch_body(zt_ref, nz_ref, pos_ref, h2_ref, xs_ref, zeros_ref, sem, *, td):
    tile_rows = ROW_TILE * ROW_BLOCKS

    @pl.when(pl.program_id(0) == 0)
    def _():
        zeros_ref[...] = jnp.zeros_like(zeros_ref)
        n = nz_ref[0]

        def zstart(z, _):
            _row_copy(zeros_ref, xs_ref.at[pl.ds(pl.multiple_of(zt_ref[z] * tile_rows, tile_rows), tile_rows)],
                      sem).start()
            return 0

        def zwait(z, _):
            _row_copy(zeros_ref, xs_ref.at[pl.ds(0, tile_rows)], sem).wait()
            return 0

        lax.fori_loop(0, n, zstart, 0)
        lax.fori_loop(0, n, zwait, 0)

    def start(r, _):
        for kk in range(2):
            _row_copy(_token_tile(h2_ref, r), _token_tile(xs_ref, pos_ref[0, kk, r]), sem).start(priority=kk)
        return 0

    lax.fori_loop(0, td, start, 0, unroll=ROW_UNROLL)
    for kk in range(2):
        _row_copy(h2_ref, xs_ref.at[pl.ds(0, td * ROW_BLOCKS)], sem).wait()


def _dispatch(zero_tiles, n_zero, pos, h2, n_rows, td):
    t = h2.shape[0] // ROW_BLOCKS
    grid_spec = pltpu.PrefetchScalarGridSpec(
        num_scalar_prefetch=2,
        grid=(t // td,),
        in_specs=[pl.BlockSpec((1, 2, td), lambda i, zt, nz: (i, 0, 0), memory_space=pltpu.SMEM),
                  pl.BlockSpec((td * ROW_BLOCKS, LANES), lambda i, zt, nz: (i, 0))],
        out_specs=pl.BlockSpec(memory_space=pl.ANY),
        scratch_shapes=[pltpu.VMEM((ROW_TILE * ROW_BLOCKS, LANES), F32), pltpu.SemaphoreType.DMA(())],
    )
    return pl.pallas_call(
        functools.partial(_dispatch_body, td=td),
        grid_spec=grid_spec,
        out_shape=jax.ShapeDtypeStruct((n_rows * ROW_BLOCKS, LANES), F32),
        compiler_params=pltpu.CompilerParams(dimension_semantics=("arbitrary",)),
        name="moe_dispatch",
    )(zero_tiles, n_zero, pos, h2)


def _expert_body(te_ref, tv_ref, xi_ref, xs_ref, wg_ref, wu_ref, wd_ref, ys_ref, wgb_ref, wub_ref, wdb_ref):
    i = pl.program_id(0)
    prev = te_ref[jnp.maximum(i - 1, 0)]

    @pl.when((i == 0) | (te_ref[i] != prev))
    def _():
        wgb_ref[...] = wg_ref[0, 0].astype(BF16)
        wub_ref[...] = wu_ref[0, 0].astype(BF16)
        wdb_ref[...] = wd_ref[0, 0].astype(BF16)

    @pl.when(tv_ref[i] == 1)
    def _():
        xb = _from_token_tiles(xs_ref, ROW_TILE).astype(BF16)
        hidden = jax.nn.silu(_dot(xb, wgb_ref[...])) * _dot(xb, wub_ref[...])
        _to_token_tiles(ys_ref, _dot(hidden.astype(BF16), wdb_ref[...]), ROW_TILE)

    @pl.when(tv_ref[i] == 0)
    def _():
        ys_ref[...] = jnp.zeros_like(ys_ref)


def _experts(tile_expert, tile_valid, tile_src, xs, wg, wu, wd, layer):
    tile_rows = ROW_TILE * ROW_BLOCKS
    grid_spec = pltpu.PrefetchScalarGridSpec(
        num_scalar_prefetch=3,
        grid=(xs.shape[0] // tile_rows,),
        in_specs=[pl.BlockSpec((tile_rows, LANES), lambda i, te, tv, xi: (xi[i], 0)),
                  pl.BlockSpec((1, 1, D_MODEL, D_EXPERT), lambda i, te, tv, xi: (layer, te[i], 0, 0)),
                  pl.BlockSpec((1, 1, D_MODEL, D_EXPERT), lambda i, te, tv, xi: (layer, te[i], 0, 0)),
                  pl.BlockSpec((1, 1, D_EXPERT, D_MODEL), lambda i, te, tv, xi: (layer, te[i], 0, 0))],
        out_specs=pl.BlockSpec((tile_rows, LANES), lambda i, te, tv, xi: (i, 0)),
        scratch_shapes=[pltpu.VMEM((D_MODEL, D_EXPERT), BF16), pltpu.VMEM((D_MODEL, D_EXPERT), BF16),
                        pltpu.VMEM((D_EXPERT, D_MODEL), BF16)],
    )
    return pl.pallas_call(
        _expert_body,
        grid_spec=grid_spec,
        out_shape=jax.ShapeDtypeStruct(xs.shape, F32),
        compiler_params=pltpu.CompilerParams(dimension_semantics=("arbitrary",),
                                             vmem_limit_bytes=VMEM_LIMIT),
        name="moe_experts",
    )(tile_expert, tile_valid, tile_src, xs, wg, wu, wd)


def _combine_body(pos_ref, x_ref, route_ref, ys_ref, out_ref, buf_ref, sem, *, tc):
    def start(r, _):
        for kk in range(2):
            _row_copy(_token_tile(ys_ref, pos_ref[0, kk, r]), _token_tile(buf_ref.at[kk], r),
                      sem).start(priority=kk)
        return 0

    lax.fori_loop(0, tc, start, 0, unroll=ROW_UNROLL)
    for kk in range(2):
        _row_copy(ys_ref.at[pl.ds(0, tc * ROW_BLOCKS)], buf_ref.at[kk], sem).wait()
    route = route_ref[...]
    out_ref[...] = (x_ref[...] + route[:, 0:1] * _from_token_tiles(buf_ref.at[0], tc)
                    + route[:, 1:2] * _from_token_tiles(buf_ref.at[1], tc))


def _combine(pos, x, route, ys, tc):
    t = x.shape[0]
    return pl.pallas_call(
        functools.partial(_combine_body, tc=tc),
        grid=(t // tc,),
        in_specs=[pl.BlockSpec((1, 2, tc), lambda i: (i, 0, 0), memory_space=pltpu.SMEM),
                  pl.BlockSpec((tc, D_MODEL), lambda i: (i, 0)),
                  pl.BlockSpec((tc, LANES), lambda i: (i, 0)),
                  pl.BlockSpec(memory_space=pl.ANY)],
        out_specs=pl.BlockSpec((tc, D_MODEL), lambda i: (i, 0)),
        out_shape=jax.ShapeDtypeStruct((t, D_MODEL), F32),
        scratch_shapes=[pltpu.VMEM((2, tc * ROW_BLOCKS, LANES), F32), pltpu.SemaphoreType.DMA(())],
        compiler_params=pltpu.CompilerParams(dimension_semantics=("arbitrary",)),
        name="moe_combine",
    )(pos, x, route, ys)


def _routing_plan(route_t, counts, n_tiles, td):
    t = route_t.shape[1]
    cnt = counts[0, N_GROUPS:N_GROUPS + N_EXPERTS].astype(jnp.int32)
    padded = ((cnt + ROW_TILE - 1) // ROW_TILE) * ROW_TILE
    ends = jnp.cumsum(padded)
    offsets = ends - padded
    experts = jnp.arange(N_EXPERTS, dtype=jnp.int32)[:, None]

    def position(e_row, r_row):
        e = e_row.astype(jnp.int32)[None, :]
        return jnp.sum(jnp.where(e == experts, offsets[:, None], 0), axis=0) + r_row.astype(jnp.int32)

    pos = jnp.stack([position(route_t[2], route_t[4]).reshape(t // td, td),
                     position(route_t[3], route_t[5]).reshape(t // td, td)], axis=1)

    used = ends[-1] // ROW_TILE
    tiles = jnp.arange(n_tiles, dtype=jnp.int32)
    valid = tiles < used
    last = jnp.maximum(used - 1, 0)
    expert_of = jnp.sum((ends[None, :] <= (tiles * ROW_TILE)[:, None]).astype(jnp.int32), axis=1)
    tile_expert = jnp.where(valid, expert_of, jnp.sum(jnp.where(tiles == last, expert_of, 0)))
    tile_src = jnp.minimum(tiles, last)
    last_tile = ends // ROW_TILE - 1
    has_pad = (cnt > 0) & (padded > cnt)
    zero_tiles = jnp.concatenate([jnp.where(has_pad, last_tile, -1).astype(jnp.int32),
                                  jnp.where(valid, -1, tiles)])
    order = jnp.argsort(zero_tiles < 0, stable=True)
    zero_tiles = zero_tiles[order]
    n_zero = jnp.sum(zero_tiles >= 0).astype(jnp.int32).reshape(1)
    return pos, tile_expert, valid.astype(jnp.int32), tile_src, jnp.maximum(zero_tiles, 0), n_zero


def _rope_tables(positions):
    half = ROPE // 2
    inv_freq = ROPE_THETA ** (-jnp.arange(half, dtype=F32) / half)
    ang = inv_freq[:, None] * positions.astype(F32).reshape(1, -1)
    spread = lambda a: jnp.tile(a.T, (1, LANES // half))
    return spread(jnp.cos(ang)), spread(jnp.sin(ang))


def _head_slots(w, width):
    k = w.shape[0]
    w = w.reshape(k, HEADS, width)
    return jnp.pad(w, ((0, 0), (0, 0), (0, LANES - width))).reshape(k, HEADS * LANES)


def _layer_weights(l, w_in, w_uq, w_ukv, q_head_g, k_head_g, w_router_group, w_router_expert):
    wi = w_in[l]
    kr = jnp.pad(wi[:, C_KR:C_KR + ROPE], ((0, 0), (NOPE, LANES - QK_DIM)))
    win = jnp.concatenate([wi[:, :C_KR], kr, wi[:, C_KR + ROPE:]], axis=1).astype(BF16)
    wuq = _head_slots(w_uq[l], QK_DIM).astype(BF16)
    kv = w_ukv[l].reshape(KV_LORA, HEADS, NOPE + V_DIM)
    wuk = _head_slots(kv[:, :, :NOPE].reshape(KV_LORA, HEADS * NOPE), NOPE)
    wuv = _head_slots(kv[:, :, NOPE:].reshape(KV_LORA, HEADS * V_DIM), V_DIM)
    wukv = jnp.concatenate([wuk, wuv], axis=1).astype(BF16)
    pad_g = lambda g: jnp.tile(jnp.pad(g.astype(F32), (0, LANES - QK_DIM)), HEADS)[None, :]
    gq = pad_g(q_head_g[l]) * (QK_DIM ** -0.5 * math.log2(math.e))
    gk = pad_g(k_head_g[l])
    wr = jnp.concatenate([w_router_group[l], w_router_expert[l]], axis=1).astype(F32)
    wr = jnp.pad(wr, ((0, 0), (0, LANES - wr.shape[1])))
    wrh = wr.astype(BF16)
    wrl = (wr - wrh.astype(F32)).astype(BF16)
    return win, wuq, wukv, gq, gk, wrh, wrl


def kernel(x, positions, norm_mix_g, w_in, q_lora_g, w_uq, kv_lora_g, w_ukv, q_head_g, k_head_g, w_attn_proj, lam_re, lam_im, log_dt, b_re, b_im, c_re, c_im, d_skip, w_glu, w_out, norm_ffn_g, w_router_group, w_router_expert, w_exp_gate, w_exp_up, w_exp_down):
    bsz, seq, _ = x.shape
    t = bsz * seq
    depth = w_in.shape[0]
    tm = 512 if t % 512 == 0 else 256
    tk = tm
    tq = 256
    bb = 2 if bsz % 2 == 0 else 1
    n_tiles = (2 * t) // ROW_TILE + N_EXPERTS
    n_rows = n_tiles * ROW_TILE
    row1 = lambda g: g.astype(F32)[None, :]

    cos_t, sin_t = _rope_tables(positions)
    xf = x.reshape(t, D_MODEL).astype(F32)
    for l in range(depth):
        win, wuq, wukv, gq, gk, wrh, wrl = _layer_weights(
            l, w_in, w_uq, w_ukv, q_head_g, k_head_g, w_router_group, w_router_expert)
        q, k, v, u, gates = _k1(xf, row1(norm_mix_g[l]), win, row1(q_lora_g[l]), wuq,
                                row1(kv_lora_g[l]), wukv, gq, gk, cos_t, sin_t, tk)
        o = _attention(q.reshape(bsz, seq, -1), k.reshape(bsz, seq, -1), v.reshape(bsz, seq, -1), tq)
        tables = _ssm_tables(lam_re[l], lam_im[l], log_dt[l], b_re[l], b_im[l], c_re[l], c_im[l], d_skip[l])
        ys = _ssm(u.reshape(bsz, seq, SSM_WIDTH), *tables, bb)
        xn, h2, route, route_t, counts = _k4(xf, o.reshape(t, -1), ys.reshape(t, SSM_WIDTH), gates,
                                             w_attn_proj[l].astype(BF16), w_glu[l].astype(BF16),
                                             w_out[l].astype(BF16), row1(norm_ffn_g[l]), wrh, wrl, tk)
        pos, tile_expert, tile_valid, tile_src, zero_tiles, n_zero = _routing_plan(route_t, counts, n_tiles, tm)
        xs = _dispatch(zero_tiles, n_zero, pos, h2, n_rows, tm)
        ye = _experts(tile_expert, tile_valid, tile_src, xs, w_exp_gate, w_exp_up, w_exp_down, l)
        xf = _combine(pos, xn, route, ye, tm)
    return xf.reshape(bsz, seq, D_MODEL).astype(x.dtype)
```

```python
import functools
import math

import jax
import jax.numpy as jnp
from jax import lax
from jax.experimental import pallas as pl
from jax.experimental.pallas import tpu as pltpu

F32 = jnp.float32
BF16 = jnp.bfloat16

D_MODEL = 1024
CHUNK = 64
HEADS = 8
NOPE = 64
ROPE = 32
QK_DIM = NOPE + ROPE
V_DIM = 64
Q_LORA = 384
KV_LORA = 256
ROPE_THETA = 10000.0
SSM_WIDTH = 512
SSM_GROUP = 16
SSM_GROUPS = 32
SSM_STATE = 64
N_GROUPS = 4
PER_GROUP = 8
N_EXPERTS = 32
D_EXPERT = 256
EPS = 1e-6

LANES = 128
SUBLANES = 8
SSM_CHUNK = 16
N_PAIR = SSM_CHUNK // 2
SSM_QBLOCKS = SSM_WIDTH // LANES
GROUPS_PER_QBLOCK = LANES // SSM_GROUP
STATE_COLS = GROUPS_PER_QBLOCK * 2 * SSM_STATE
ROW_TILE = 512
ROW_UNROLL = 8
K1_SLABS = 1
K4_SLABS = 2
ROW_BLOCKS = D_MODEL // LANES
CHUNK_ROWS = SUBLANES
CHUNKS_PER_TILE = ROW_TILE // CHUNK_ROWS
VMEM_LIMIT = 56 * 1024 * 1024


def _local_rows(tm):
    return 2 * tm + N_EXPERTS * CHUNK_ROWS

C_Q = 0
C_KV = C_Q + Q_LORA
C_KR = C_KV + KV_LORA
C_U = C_KR + LANES
C_GATE = C_U + SSM_WIDTH
IN_COLS_PAD = C_GATE + 2 * D_MODEL


def _rms(x, g):
    return x * lax.rsqrt(jnp.mean(x * x, axis=-1, keepdims=True) + EPS) * g


def _dot(a, b):
    return jnp.dot(a, b, preferred_element_type=F32)


def _to_token_tiles(ref, x, n, base=0):
    for k in range(ROW_BLOCKS):
        ref[pl.ds(base * ROW_BLOCKS + k, n, stride=ROW_BLOCKS), :] = x[:, k * LANES:(k + 1) * LANES]


def _from_token_tiles(ref, n):
    return jnp.concatenate([ref[pl.ds(k, n, stride=ROW_BLOCKS), :] for k in range(ROW_BLOCKS)], axis=1)


def _k1_body(x_ref, gmix_ref, win_ref, gql_ref, wuq_ref, gkvl_ref, wukv_ref, gq_ref, gk_ref,
             cos_ref, sin_ref, q_ref, k_ref, v_ref, u_ref, gates_ref):
    vlane = lax.broadcasted_iota(jnp.int32, (1, HEADS * LANES), 1)
    ones_col = ((vlane % LANES) == V_DIM).astype(F32)
    ts = x_ref.shape[0] // K1_SLABS
    for s in range(K1_SLABS):
        rows = pl.ds(s * ts, ts)
        hb = _rms(x_ref[rows, :], gmix_ref[...]).astype(BF16)

        def proj(lo, hi):
            return _dot(hb, win_ref[:, lo:hi])

        lane = lax.broadcasted_iota(jnp.int32, (ts, LANES), 1)
        sin = sin_ref[rows, :]
        cos_t = jnp.where(lane < NOPE, 1.0, jnp.where(lane < QK_DIM, cos_ref[rows, :], 0.0))
        sin_m = jnp.where((lane >= NOPE) & (lane < NOPE + ROPE // 2), -sin, 0.0)
        sin_p = jnp.where((lane >= NOPE + ROPE // 2) & (lane < QK_DIM), sin, 0.0)

        def head_norm_rope(r, g):
            ms = jnp.sum(r * r, axis=-1, keepdims=True) * (1.0 / QK_DIM)
            rn = r * lax.rsqrt(ms + EPS) * g
            return (rn * cos_t + pltpu.roll(rn, LANES - ROPE // 2, 1) * sin_m
                    + pltpu.roll(rn, ROPE // 2, 1) * sin_p)

        qn = _rms(proj(C_Q, C_KV), gql_ref[...]).astype(BF16)
        qf = _dot(qn, wuq_ref[...])
        for h in range(HEADS):
            sl = slice(h * LANES, (h + 1) * LANES)
            q_ref[rows, sl] = head_norm_rope(qf[:, sl], gq_ref[:, sl]).astype(BF16)

        kvn = _rms(proj(C_KV, C_KR), gkvl_ref[...]).astype(BF16)
        kf = _dot(kvn, wukv_ref[...])
        kr = proj(C_KR, C_U)
        for h in range(HEADS):
            sl = slice(h * LANES, (h + 1) * LANES)
            k_ref[rows, sl] = head_norm_rope(kf[:, sl] + kr, gk_ref[:, sl]).astype(BF16)
        v_ref[rows, :] = (kf[:, HEADS * LANES:] + ones_col).astype(BF16)

        u_ref[rows, :] = proj(C_U, C_GATE)
        gates_ref[rows, :] = jax.nn.sigmoid(proj(C_GATE, IN_COLS_PAD)).astype(BF16)


def _k1(x, gmix, win, gql, wuq, gkvl, wukv, gq, gk, cos_t, sin_t, tm):
    t = x.shape[0]
    row = lambda w: pl.BlockSpec((tm, w), lambda i: (i, 0))
    full = lambda a: pl.BlockSpec(a.shape, lambda i: (0,) * a.ndim)
    return pl.pallas_call(
        _k1_body,
        grid=(t // tm,),
        in_specs=[row(D_MODEL), full(gmix), full(win), full(gql), full(wuq), full(gkvl), full(wukv),
                  full(gq), full(gk), row(LANES), row(LANES)],
        out_specs=[row(HEADS * LANES), row(HEADS * LANES), row(HEADS * LANES), row(SSM_WIDTH),
                   row(2 * D_MODEL)],
        out_shape=[jax.ShapeDtypeStruct((t, HEADS * LANES), BF16),
                   jax.ShapeDtypeStruct((t, HEADS * LANES), BF16),
                   jax.ShapeDtypeStruct((t, HEADS * LANES), BF16),
                   jax.ShapeDtypeStruct((t, SSM_WIDTH), F32),
                   jax.ShapeDtypeStruct((t, 2 * D_MODEL), BF16)],
        compiler_params=pltpu.CompilerParams(dimension_semantics=("arbitrary",),
                                             vmem_limit_bytes=VMEM_LIMIT),
        name="k1_inproj",
    )(x, gmix, win, gql, wuq, gkvl, wukv, gq, gk, cos_t, sin_t)


def _attn_body(q_ref, k_ref, v_ref, o_ref, *, tq, nq):
    i = pl.program_id(1)
    row_chunk = lax.broadcasted_iota(jnp.int32, (tq, tq), 0) // CHUNK
    col_chunk = lax.broadcasted_iota(jnp.int32, (tq, tq), 1) // CHUNK
    diag_mask = col_chunk <= row_chunk
    neg = jnp.float32(-1e30)

    def scores(hh, n_full):
        sl = slice(hh * LANES, (hh + 1) * LANES)
        s = lax.dot_general(q_ref[0, :, sl], k_ref[0, :n_full + tq, sl], (((1,), (1,)), ((), ())),
                            preferred_element_type=F32)
        s_diag = jnp.where(diag_mask, s[:, n_full:], neg)
        return jnp.concatenate([s[:, :n_full], s_diag], axis=1) if n_full else s_diag

    def probs(s):
        return jnp.exp2(s - jnp.max(s, axis=-1, keepdims=True)).astype(BF16)

    def values(p, hh, n_full):
        of = _dot(p, v_ref[0, :n_full + tq, hh * LANES:(hh + 1) * LANES])
        return of * (1.0 / of[:, V_DIM:V_DIM + 1])

    for ii in range(nq):
        @pl.when(i == ii)
        def _(ii=ii):
            n_full = ii * tq
            lane = lax.broadcasted_iota(jnp.int32, (tq, LANES), 1)
            ss = [scores(hh, n_full) for hh in range(HEADS)]
            ps = [probs(s) for s in ss]
            os_ = [values(p, hh, n_full) for hh, p in enumerate(ps)]
            for hp in range(HEADS // 2):
                o_ref[0, :, hp * LANES:(hp + 1) * LANES] = jnp.where(
                    lane < V_DIM, os_[2 * hp], pltpu.roll(os_[2 * hp + 1], V_DIM, 1)).astype(BF16)


def _attention(q, k, v, tq):
    b, l, _ = q.shape
    nq = l // tq
    return pl.pallas_call(
        functools.partial(_attn_body, tq=tq, nq=nq),
        grid=(b, nq),
        in_specs=[pl.BlockSpec((1, tq, HEADS * LANES), lambda bi, i: (bi, i, 0)),
                  pl.BlockSpec((1, l, HEADS * LANES), lambda bi, i: (bi, 0, 0)),
                  pl.BlockSpec((1, l, HEADS * LANES), lambda bi, i: (bi, 0, 0))],
        out_specs=pl.BlockSpec((1, tq, HEADS * V_DIM), lambda bi, i: (bi, i, 0)),
        out_shape=jax.ShapeDtypeStruct((b, l, HEADS * V_DIM), BF16),
        compiler_params=pltpu.CompilerParams(
            dimension_semantics=("arbitrary", "arbitrary"), vmem_limit_bytes=VMEM_LIMIT),
        name="attention",
    )(q, k, v)


def _cmul(a, b):
    return a[0] * b[0] - a[1] * b[1], a[0] * b[1] + a[1] * b[0]


def _ssm_tables(lam_re, lam_im, log_dt, b_re, b_im, c_re, c_im, d_skip):
    lc = SSM_CHUNK
    nq, gq = SSM_QBLOCKS, GROUPS_PER_QBLOCK
    lr, li = lam_re.astype(F32), lam_im.astype(F32)
    dt = jnp.exp(log_dt.astype(F32))[:, None]
    steps = jnp.arange(lc + 1, dtype=F32)[:, None, None]
    mag = jnp.exp(lr * dt * steps)
    pw = (mag * jnp.cos(li * dt * steps), mag * jnp.sin(li * dt * steps))
    num = (pw[0][1] - 1.0, pw[1][1])
    den = lr * lr + li * li
    ratio = ((num[0] * lr + num[1] * li) / den, (num[1] * lr - num[0] * li) / den)
    b_bar = _cmul((ratio[0][..., None], ratio[1][..., None]), (b_re.astype(F32), b_im.astype(F32)))
    c = (c_re.astype(F32), c_im.astype(F32))

    m = _cmul((pw[0][:lc, :, :, None], pw[1][:lc, :, :, None]), (b_bar[0][None], b_bar[1][None]))
    kern = jnp.einsum('gip,dgpj->dgij', c[0], m[0]) - jnp.einsum('gip,dgpj->dgij', c[1], m[1])
    kern = kern.at[0].add(jnp.eye(SSM_GROUP, dtype=F32)[None] * d_skip.astype(F32)[:, :, None])
    kpad = jnp.concatenate([jnp.zeros_like(kern[:1]), kern], axis=0)
    big_d = jnp.arange(N_PAIR)
    rows = []
    for s2 in range(2):
        cols = []
        for t2 in range(2):
            kd = kpad[2 * big_d + t2 - s2 + 1].reshape(N_PAIR, nq, gq, SSM_GROUP, SSM_GROUP)
            cols.append(jnp.transpose(kd, (1, 0, 4, 2, 3)))
        rows.append(jnp.stack(cols, axis=3))
    tab_t = jnp.stack(rows, axis=2).reshape(nq, N_PAIR, 2, SSM_GROUP, 2 * LANES)

    bt = (jnp.swapaxes(b_bar[0], 1, 2), jnp.swapaxes(b_bar[1], 1, 2))
    vin = _cmul((pw[0][:lc][::-1][:, :, None, :], pw[1][:lc][::-1][:, :, None, :]), (bt[0][None], bt[1][None]))
    vin = jnp.concatenate(vin, axis=-1).reshape(N_PAIR, 2, nq, gq, SSM_GROUP, 2 * SSM_STATE)
    tab_in = jnp.transpose(vin, (2, 0, 1, 4, 3, 5)).reshape(nq, N_PAIR, 2, SSM_GROUP, STATE_COLS)

    cw = _cmul((c[0][None], c[1][None]), (pw[0][1:lc + 1][:, :, None, :], pw[1][1:lc + 1][:, :, None, :]))
    cw = jnp.concatenate([cw[0], -cw[1]], axis=-1).reshape(N_PAIR, 2, nq, gq, SSM_GROUP, 2 * SSM_STATE)
    tab_out = jnp.transpose(cw, (2, 0, 1, 3, 4, 5)).reshape(nq, N_PAIR, 2 * LANES, 2 * SSM_STATE)

    ar = pw[0][lc].reshape(nq, gq, 1, SSM_STATE)
    ai = pw[1][lc].reshape(nq, gq, 1, SSM_STATE)
    a_same = jnp.concatenate([ar, ar], axis=2).reshape(nq, 1, STATE_COLS)
    a_swap = jnp.concatenate([-ai, ai], axis=2).reshape(nq, 1, STATE_COLS)
    return tab_t, tab_in, tab_out, jnp.concatenate([a_same, a_swap], axis=1)


def _ssm_body(u_ref, tt_ref, tin_ref, tout_ref, a_ref, y_ref, wt_ref, win_ref, wout_ref, st_ref, sw_ref,
              hp_ref, *, bb, nc):
    gq = GROUPS_PER_QBLOCK

    @pl.when(pl.program_id(1) == 0)
    def _():
        grp_t = (lax.broadcasted_iota(jnp.int32, (SSM_GROUP, 2 * LANES), 1) % LANES) // SSM_GROUP
        grp_in = lax.broadcasted_iota(jnp.int32, (SSM_GROUP, STATE_COLS), 1) // LANES
        grp_out = (lax.broadcasted_iota(jnp.int32, (LANES, 2 * LANES), 1) % LANES) // SSM_GROUP
        for d in range(N_PAIR):
            for s2 in range(2):
                piece_t = tt_ref[0, d, s2]
                piece_in = tin_ref[0, d, s2]
                for gl in range(gq):
                    rows = pl.ds(s2 * LANES + gl * SSM_GROUP, SSM_GROUP)
                    wt_ref[d, rows, :] = jnp.where(grp_t == gl, piece_t, 0.0).astype(BF16)
                    win_ref[d, rows, :] = jnp.where(grp_in == gl, piece_in, 0.0).astype(BF16)
            piece_out = tout_ref[0, d].T
            for gl in range(gq):
                wout_ref[d, pl.ds(gl * LANES, LANES), :] = jnp.where(grp_out == gl, piece_out, 0.0).astype(BF16)

    xs = []
    for s in range(N_PAIR):
        per_batch = []
        for b in range(bb):
            x0 = u_ref[b, pl.ds(2 * s, nc, stride=SSM_CHUNK), :]
            x1 = u_ref[b, pl.ds(2 * s + 1, nc, stride=SSM_CHUNK), :]
            per_batch.append(jnp.concatenate([x0, x1], axis=1))
        xs.append(jnp.concatenate(per_batch, axis=0).astype(BF16))

    st = _dot(xs[0], win_ref[0])
    for s in range(1, N_PAIR):
        st = st + _dot(xs[s], win_ref[s])
    st_ref[...] = st
    sw_ref[...] = jnp.concatenate(
        [pltpu.roll(st[:, g * LANES:(g + 1) * LANES], SSM_STATE, 1) for g in range(gq)], axis=1)

    a_same = a_ref[0, 0:1, :]
    a_swap = a_ref[0, 1:2, :]

    def step(c, carry):
        new = []
        for b in range(bb):
            h, hs = carry[b]
            row = b * nc + c
            hp_ref[pl.ds(row, 1), :] = h
            new.append((a_same * h + a_swap * hs + st_ref[pl.ds(row, 1), :],
                        a_same * hs - a_swap * h + sw_ref[pl.ds(row, 1), :]))
        return tuple(new)

    zero = jnp.zeros((1, STATE_COLS), F32)
    lax.fori_loop(0, nc, step, tuple((zero, zero) for _ in range(bb)))

    hp = hp_ref[...].astype(BF16)
    for t in range(N_PAIR):
        acc = _dot(hp, wout_ref[t])
        for s in range(t + 1):
            acc = acc + _dot(xs[s], wt_ref[t - s])
        yg = jax.nn.gelu(acc)
        for b in range(bb):
            rows = slice(b * nc, (b + 1) * nc)
            y_ref[b, pl.ds(2 * t, nc, stride=SSM_CHUNK), :] = yg[rows, :LANES]
            y_ref[b, pl.ds(2 * t + 1, nc, stride=SSM_CHUNK), :] = yg[rows, LANES:]


def _ssm(u, tab_t, tab_in, tab_out, a_chunk, bb):
    b, l, _ = u.shape
    nc = l // SSM_CHUNK
    wspec = lambda a: pl.BlockSpec((1,) + a.shape[1:], lambda q, bi: (q,) + (0,) * (a.ndim - 1))
    return pl.pallas_call(
        functools.partial(_ssm_body, bb=bb, nc=nc),
        grid=(SSM_QBLOCKS, b // bb),
        in_specs=[pl.BlockSpec((bb, l, LANES), lambda q, bi: (bi, 0, q)),
                  wspec(tab_t), wspec(tab_in), wspec(tab_out), wspec(a_chunk)],
        out_specs=pl.BlockSpec((bb, l, LANES), lambda q, bi: (bi, 0, q)),
        out_shape=jax.ShapeDtypeStruct((b, l, SSM_WIDTH), F32),
        scratch_shapes=[pltpu.VMEM((N_PAIR, 2 * LANES, 2 * LANES), BF16),
                        pltpu.VMEM((N_PAIR, 2 * LANES, STATE_COLS), BF16),
                        pltpu.VMEM((N_PAIR, STATE_COLS, 2 * LANES), BF16),
                        pltpu.VMEM((bb * nc, STATE_COLS), F32),
                        pltpu.VMEM((bb * nc, STATE_COLS), F32),
                        pltpu.VMEM((bb * nc, STATE_COLS), F32)],
        compiler_params=pltpu.CompilerParams(dimension_semantics=("arbitrary", "arbitrary"),
                                             vmem_limit_bytes=VMEM_LIMIT),
        name="ssm",
    )(u, tab_t, tab_in, tab_out, a_chunk)


def _k4_body(x_ref, o_ref, ys_ref, g_ref, wap_ref, wglu_ref, wout_ref, gffn_ref, wrh_ref, wrl_ref,
             xn_ref, xsl_ref, route_ref, routet_ref, cnt_ref, *, tm):
    ts = tm // K4_SLABS
    slabs = [pl.ds(s * ts, ts) for s in range(K4_SLABS)]

    logits_all = []
    h2b_all = []
    for s, rows in enumerate(slabs):
        y_attn = _dot(o_ref[rows, :], wap_ref[...])
        vg = _dot(ys_ref[rows, :].astype(BF16), wglu_ref[...])
        y_ssm = vg[:, :D_MODEL] * jax.nn.sigmoid(vg[:, D_MODEL:])
        g = g_ref[rows, :].astype(F32)
        merged = g[:, :D_MODEL] * y_attn + g[:, D_MODEL:] * y_ssm
        xn = x_ref[rows, :] + _dot(merged.astype(BF16), wout_ref[...])
        xn_ref[rows, :] = xn
        h2 = _rms(xn, gffn_ref[...])
        hh = h2.astype(BF16)
        hl = (h2 - hh.astype(F32)).astype(BF16)
        h2b_all.append(hh)
        logits_all.append(_dot(hh, wrh_ref[...]) + _dot(hh, wrl_ref[...]) + _dot(hl, wrh_ref[...]))

    lane = lax.broadcasted_iota(jnp.int32, (ts, LANES), 1).astype(F32)
    ninf = jnp.float32(-jnp.inf)
    big = jnp.float32(4 * LANES)
    tri = (lax.broadcasted_iota(jnp.int32, (ts, ts), 1)
           < lax.broadcasted_iota(jnp.int32, (ts, ts), 0)).astype(BF16)
    seen = jnp.zeros((1, LANES), F32)
    picks = []
    for s, rows in enumerate(slabs):
        logits = logits_all[s]

        def top(mask):
            val = jnp.max(jnp.where(mask, logits, ninf), axis=-1, keepdims=True)
            idx = jnp.min(jnp.where(mask & (logits == val), lane, big), axis=-1, keepdims=True)
            return val, idx

        gmask = lane < N_GROUPS
        gmax, gidx = top(gmask)
        g_w = 1.0 / jnp.sum(jnp.where(gmask, jnp.exp(logits - gmax), 0.0), axis=-1, keepdims=True)
        lo = N_GROUPS + PER_GROUP * gidx
        emask = (lane >= lo) & (lane < lo + PER_GROUP)
        l1, i1 = top(emask)
        l2, i2 = top(emask & (lane != i1))
        e21 = jnp.exp(l2 - l1)
        w1 = g_w / (1.0 + e21)
        w2 = g_w * e21 / (1.0 + e21)

        hot1 = lane == i1
        hot2 = lane == i2
        onehot = (hot1 | hot2).astype(BF16)
        picks.append((w1, w2, i1, i2, hot1, hot2, _dot(tri, onehot) + seen))
        seen = seen + jnp.sum(onehot.astype(F32), axis=0, keepdims=True)
    cnt_ref[0] = seen

    padded = jnp.floor((seen + (CHUNK_ROWS - 1)) * (1.0 / CHUNK_ROWS)) * CHUNK_ROWS
    earlier = (lax.broadcasted_iota(jnp.int32, (LANES, LANES), 0)
               < lax.broadcasted_iota(jnp.int32, (LANES, LANES), 1)).astype(BF16)
    start = _dot(jnp.broadcast_to(padded, (SUBLANES, LANES)).astype(BF16), earlier)[0:1, :]

    local_rows = []
    for s, rows in enumerate(slabs):
        w1, w2, i1, i2, hot1, hot2, before = picks[s]
        r1 = jnp.sum(jnp.where(hot1, before + start, 0.0), axis=-1, keepdims=True)
        r2 = jnp.sum(jnp.where(hot2, before + start, 0.0), axis=-1, keepdims=True)
        cols = (w1, w2, i1 - N_GROUPS, i2 - N_GROUPS, r1, r2)
        out = jnp.zeros((ts, LANES), F32)
        for n, col in enumerate(cols):
            out = jnp.where(lane == n, col, out)
        route_ref[rows, :] = out
        out_t = out.T[:SUBLANES, :]
        routet_ref[:, rows] = out_t
        local_rows.append(out_t[4:6, :])

    lr = jnp.concatenate(local_rows, axis=1)
    n_local = xsl_ref.shape[0]
    row_id = lax.broadcasted_iota(jnp.int32, (n_local, tm), 0).astype(F32)
    select = ((row_id == lr[0:1, :]) | (row_id == lr[1:2, :])).astype(BF16)
    xsl_ref[...] = _dot(select, jnp.concatenate(h2b_all, axis=0))


def _k4(x, o, ys, gates, wap, wglu, wout, gffn, wrh, wrl, tm):
    t = x.shape[0]
    row = lambda w: pl.BlockSpec((tm, w), lambda i: (i, 0))
    full = lambda a: pl.BlockSpec(a.shape, lambda i: (0,) * a.ndim)
    return pl.pallas_call(
        functools.partial(_k4_body, tm=tm),
        grid=(t // tm,),
        in_specs=[row(D_MODEL), row(HEADS * V_DIM), row(SSM_WIDTH), row(2 * D_MODEL),
                  full(wap), full(wglu), full(wout), full(gffn), full(wrh), full(wrl)],
        out_specs=[row(D_MODEL), pl.BlockSpec((_local_rows(tm), D_MODEL), lambda i: (i, 0)), row(LANES),
                   pl.BlockSpec((SUBLANES, tm), lambda i: (0, i)),
                   pl.BlockSpec((1, 1, LANES), lambda i: (i, 0, 0))],
        out_shape=[jax.ShapeDtypeStruct((t, D_MODEL), F32),
                   jax.ShapeDtypeStruct((t // tm * _local_rows(tm), D_MODEL), F32),
                   jax.ShapeDtypeStruct((t, LANES), F32), jax.ShapeDtypeStruct((SUBLANES, t), F32),
                   jax.ShapeDtypeStruct((t // tm, 1, LANES), F32)],
        compiler_params=pltpu.CompilerParams(dimension_semantics=("arbitrary",),
                                             vmem_limit_bytes=VMEM_LIMIT),
        name="k4_merge_router",
    )(x, o, ys, gates, wap, wglu, wout, gffn, wrh, wrl)


def _row_copy(src, dst, sem):
    return pltpu.make_async_copy(src, dst, sem)


def _token_tile(ref, r):
    return ref.at[pl.ds(pl.multiple_of(r * ROW_BLOCKS, ROW_BLOCKS), ROW_BLOCKS)]


def _expert_body(te_ref, tv_ref, cs_ref, xsl_ref, wg_ref, wu_ref, wd_ref, ys_ref,
                 wgb_ref, wub_ref, wdb_ref, xbuf_ref, sem, *, n_steps):
    i = pl.program_id(0)
    slot = lax.rem(i, 2)

    def fetch(tile, into):
        def start(c, _):
            chunk = cs_ref[tile * CHUNKS_PER_TILE + c]
            src = xsl_ref.at[pl.ds(pl.multiple_of(chunk * CHUNK_ROWS, CHUNK_ROWS), CHUNK_ROWS)]
            dst = xbuf_ref.at[into, pl.ds(pl.multiple_of(c * CHUNK_ROWS, CHUNK_ROWS), CHUNK_ROWS)]
            _row_copy(src, dst, sem.at[into]).start()
            return 0
        lax.fori_loop(0, CHUNKS_PER_TILE, start, 0, unroll=ROW_UNROLL)

    @pl.when(i == 0)
    def _():
        fetch(0, 0)

    @pl.when(i + 1 < n_steps)
    def _():
        fetch(i + 1, 1 - slot)

    prev = te_ref[jnp.maximum(i - 1, 0)]

    @pl.when((i == 0) | (te_ref[i] != prev))
    def _():
        wgb_ref[...] = wg_ref[0, 0].astype(BF16)
        wub_ref[...] = wu_ref[0, 0].astype(BF16)
        wdb_ref[...] = wd_ref[0, 0].astype(BF16)

    _row_copy(xsl_ref.at[pl.ds(0, ROW_TILE)], xbuf_ref.at[slot], sem.at[slot]).wait()

    @pl.when(tv_ref[i] == 1)
    def _():
        xb = xbuf_ref[slot].astype(BF16)
        hidden = jax.nn.silu(_dot(xb, wgb_ref[...])) * _dot(xb, wub_ref[...])
        _to_token_tiles(ys_ref, _dot(hidden.astype(BF16), wdb_ref[...]), ROW_TILE)

    @pl.when(tv_ref[i] == 0)
    def _():
        ys_ref[...] = jnp.zeros_like(ys_ref)


def _experts(tile_expert, tile_valid, chunk_src, xsl, wg, wu, wd, layer):
    n_steps = tile_expert.shape[0]
    tile_rows = ROW_TILE * ROW_BLOCKS
    grid_spec = pltpu.PrefetchScalarGridSpec(
        num_scalar_prefetch=3,
        grid=(n_steps,),
        in_specs=[pl.BlockSpec(memory_space=pl.ANY),
                  pl.BlockSpec((1, 1, D_MODEL, D_EXPERT), lambda i, te, tv, cs: (layer, te[i], 0, 0)),
                  pl.BlockSpec((1, 1, D_MODEL, D_EXPERT), lambda i, te, tv, cs: (layer, te[i], 0, 0)),
                  pl.BlockSpec((1, 1, D_EXPERT, D_MODEL), lambda i, te, tv, cs: (layer, te[i], 0, 0))],
        out_specs=pl.BlockSpec((tile_rows, LANES), lambda i, te, tv, cs: (i, 0)),
        scratch_shapes=[pltpu.VMEM((D_MODEL, D_EXPERT), BF16), pltpu.VMEM((D_MODEL, D_EXPERT), BF16),
                        pltpu.VMEM((D_EXPERT, D_MODEL), BF16), pltpu.VMEM((2, ROW_TILE, D_MODEL), F32),
                        pltpu.SemaphoreType.DMA((2,))],
    )
    return pl.pallas_call(
        functools.partial(_expert_body, n_steps=n_steps),
        grid_spec=grid_spec,
        out_shape=jax.ShapeDtypeStruct((n_steps * tile_rows, LANES), F32),
        compiler_params=pltpu.CompilerParams(dimension_semantics=("arbitrary",),
                                             vmem_limit_bytes=VMEM_LIMIT),
        name="moe_experts",
    )(tile_expert, tile_valid, chunk_src, xsl, wg, wu, wd)


def _combine_body(pos_ref, x_ref, route_ref, ys_ref, out_ref, buf_ref, sem, *, tc):
    def start(r, _):
        for kk in range(2):
            _row_copy(_token_tile(ys_ref, pos_ref[0, kk, r]), _token_tile(buf_ref.at[kk], r),
                      sem).start(priority=kk)
        return 0

    lax.fori_loop(0, tc, start, 0, unroll=ROW_UNROLL)
    for kk in range(2):
        _row_copy(ys_ref.at[pl.ds(0, tc * ROW_BLOCKS)], buf_ref.at[kk], sem).wait()
    route = route_ref[...]
    out_ref[...] = (x_ref[...] + route[:, 0:1] * _from_token_tiles(buf_ref.at[0], tc)
                    + route[:, 1:2] * _from_token_tiles(buf_ref.at[1], tc))


def _combine(pos, x, route, ys, tc):
    t = x.shape[0]
    return pl.pallas_call(
        functools.partial(_combine_body, tc=tc),
        grid=(t // tc,),
        in_specs=[pl.BlockSpec((1, 2, tc), lambda i: (i, 0, 0), memory_space=pltpu.SMEM),
                  pl.BlockSpec((tc, D_MODEL), lambda i: (i, 0)),
                  pl.BlockSpec((tc, LANES), lambda i: (i, 0)),
                  pl.BlockSpec(memory_space=pl.ANY)],
        out_specs=pl.BlockSpec((tc, D_MODEL), lambda i: (i, 0)),
        out_shape=jax.ShapeDtypeStruct((t, D_MODEL), F32),
        scratch_shapes=[pltpu.VMEM((2, tc * ROW_BLOCKS, LANES), F32), pltpu.SemaphoreType.DMA(())],
        compiler_params=pltpu.CompilerParams(dimension_semantics=("arbitrary",)),
        name="moe_combine",
    )(pos, x, route, ys)


def _expert_tiles(t, tm):
    return (2 * t + (t // tm) * N_EXPERTS * (CHUNK_ROWS - 1)) // ROW_TILE + N_EXPERTS


def _routing_plan(route_t, counts, tm):
    i32 = jnp.int32
    t = route_t.shape[1]
    n_tok_tiles = t // tm
    local_chunks = _local_rows(tm) // CHUNK_ROWS
    n_tiles = _expert_tiles(t, tm)
    cnt = counts[:, 0, N_GROUPS:N_GROUPS + N_EXPERTS].astype(i32)
    chunks = (cnt + CHUNK_ROWS - 1) // CHUNK_ROWS
    first_local = jnp.cumsum(chunks, axis=1) - chunks
    upto = jnp.cumsum(chunks, axis=0)
    before = upto - chunks
    n_chunks = upto[-1]
    n_exp_tiles = (n_chunks + CHUNKS_PER_TILE - 1) // CHUNKS_PER_TILE
    tile_end = jnp.cumsum(n_exp_tiles)
    tile_start = tile_end - n_exp_tiles
    used = tile_end[-1]
    tiles = jnp.arange(n_tiles, dtype=i32)
    valid = tiles < used
    last = jnp.maximum(used - 1, 0)
    expert_of = jnp.sum((tile_end[None, :] <= tiles[:, None]).astype(i32), axis=1)
    tile_expert = jnp.where(valid, expert_of, jnp.sum(jnp.where(tiles == last, expert_of, 0)))

    sel = tile_expert[:, None] == jnp.arange(N_EXPERTS, dtype=i32)[None, :]
    per_tile = lambda table: jnp.sum(jnp.where(sel[:, :, None], table.T[None], 0), axis=1)
    per_tile1 = lambda vec: jnp.sum(jnp.where(sel, vec[None, :], 0), axis=1)
    upto_g, before_g, first_g = per_tile(upto), per_tile(before), per_tile(first_local)
    k = ((tiles - per_tile1(tile_start))[:, None] * CHUNKS_PER_TILE
         + jnp.arange(CHUNKS_PER_TILE, dtype=i32)[None, :])
    k_valid = valid[:, None] & (k < per_tile1(n_chunks)[:, None])
    src_tile = jnp.minimum(jnp.sum((upto_g[:, None, :] <= k[:, :, None]).astype(i32), axis=2), n_tok_tiles - 1)
    at = src_tile[:, :, None] == jnp.arange(n_tok_tiles, dtype=i32)[None, None, :]
    before_at = jnp.sum(jnp.where(at, before_g[:, None, :], 0), axis=2)
    first_at = jnp.sum(jnp.where(at, first_g[:, None, :], 0), axis=2)
    chunk = src_tile * local_chunks + first_at + (k - before_at)
    chunk_src = jnp.where(k_valid, chunk, local_chunks - 1).reshape(-1).astype(i32)

    base = tile_start[None, :] * ROW_TILE + CHUNK_ROWS * (before - first_local)
    base_t = jnp.repeat(base.T, tm, axis=1)
    experts = jnp.arange(N_EXPERTS, dtype=i32)[:, None]

    def position(e_row, local_row):
        e = e_row.astype(i32)[None, :]
        return jnp.sum(jnp.where(e == experts, base_t, 0), axis=0) + local_row.astype(i32)

    pos = jnp.stack([position(route_t[2], route_t[4]).reshape(n_tok_tiles, tm),
                     position(route_t[3], route_t[5]).reshape(n_tok_tiles, tm)], axis=1)
    return pos, tile_expert, valid.astype(i32), chunk_src


def _rope_tables(positions):
    half = ROPE // 2
    inv_freq = ROPE_THETA ** (-jnp.arange(half, dtype=F32) / half)
    ang = inv_freq[:, None] * positions.astype(F32).reshape(1, -1)
    spread = lambda a: jnp.tile(a.T, (1, LANES // half))
    return spread(jnp.cos(ang)), spread(jnp.sin(ang))


def _head_slots(w, width):
    k = w.shape[0]
    w = w.reshape(k, HEADS, width)
    return jnp.pad(w, ((0, 0), (0, 0), (0, LANES - width))).reshape(k, HEADS * LANES)


def _layer_weights(l, w_in, w_uq, w_ukv, q_head_g, k_head_g, w_router_group, w_router_expert):
    wi = w_in[l]
    kr = jnp.pad(wi[:, C_KR:C_KR + ROPE], ((0, 0), (NOPE, LANES - QK_DIM)))
    win = jnp.concatenate([wi[:, :C_KR], kr, wi[:, C_KR + ROPE:]], axis=1).astype(BF16)
    wuq = _head_slots(w_uq[l], QK_DIM).astype(BF16)
    kv = w_ukv[l].reshape(KV_LORA, HEADS, NOPE + V_DIM)
    wuk = _head_slots(kv[:, :, :NOPE].reshape(KV_LORA, HEADS * NOPE), NOPE)
    wuv = _head_slots(kv[:, :, NOPE:].reshape(KV_LORA, HEADS * V_DIM), V_DIM)
    wukv = jnp.concatenate([wuk, wuv], axis=1).astype(BF16)
    pad_g = lambda g: jnp.tile(jnp.pad(g.astype(F32), (0, LANES - QK_DIM)), HEADS)[None, :]
    gq = pad_g(q_head_g[l]) * (QK_DIM ** -0.5 * math.log2(math.e))
    gk = pad_g(k_head_g[l])
    wr = jnp.concatenate([w_router_group[l], w_router_expert[l]], axis=1).astype(F32)
    wr = jnp.pad(wr, ((0, 0), (0, LANES - wr.shape[1])))
    wrh = wr.astype(BF16)
    wrl = (wr - wrh.astype(F32)).astype(BF16)
    return win, wuq, wukv, gq, gk, wrh, wrl


def kernel(x, positions, norm_mix_g, w_in, q_lora_g, w_uq, kv_lora_g, w_ukv, q_head_g, k_head_g, w_attn_proj, lam_re, lam_im, log_dt, b_re, b_im, c_re, c_im, d_skip, w_glu, w_out, norm_ffn_g, w_router_group, w_router_expert, w_exp_gate, w_exp_up, w_exp_down):
    bsz, seq, _ = x.shape
    t = bsz * seq
    depth = w_in.shape[0]
    tm = 512 if t % 512 == 0 else 256
    tq = 256
    bb = 2 if bsz % 2 == 0 else 1
    row1 = lambda g: g.astype(F32)[None, :]

    cos_t, sin_t = _rope_tables(positions)
    xf = x.reshape(t, D_MODEL).astype(F32)
    for l in range(depth):
        win, wuq, wukv, gq, gk, wrh, wrl = _layer_weights(
            l, w_in, w_uq, w_ukv, q_head_g, k_head_g, w_router_group, w_router_expert)
        q, k, v, u, gates = _k1(xf, row1(norm_mix_g[l]), win, row1(q_lora_g[l]), wuq,
                                row1(kv_lora_g[l]), wukv, gq, gk, cos_t, sin_t, tm)
        o = _attention(q.reshape(bsz, seq, -1), k.reshape(bsz, seq, -1), v.reshape(bsz, seq, -1), tq)
        tables = _ssm_tables(lam_re[l], lam_im[l], log_dt[l], b_re[l], b_im[l], c_re[l], c_im[l], d_skip[l])
        ys = _ssm(u.reshape(bsz, seq, SSM_WIDTH), *tables, bb)
        xn, xsl, route, route_t, counts = _k4(xf, o.reshape(t, -1), ys.reshape(t, SSM_WIDTH), gates,
                                              w_attn_proj[l].astype(BF16), w_glu[l].astype(BF16),
                                              w_out[l].astype(BF16), row1(norm_ffn_g[l]), wrh, wrl, tm)
        pos, tile_expert, tile_valid, chunk_src = _routing_plan(route_t, counts, tm)
        ye = _experts(tile_expert, tile_valid, chunk_src, xsl, w_exp_gate, w_exp_up, w_exp_down, l)
        xf = _combine(pos, xn, route, ye, tm)
    return xf.reshape(bsz, seq, D_MODEL).astype(x.dtype)
```

```python
import functools
import math

import jax
import jax.numpy as jnp
from jax import lax
from jax.experimental import pallas as pl
from jax.experimental.pallas import tpu as pltpu

F32 = jnp.float32
BF16 = jnp.bfloat16

D_MODEL = 1024
CHUNK = 64
HEADS = 8
NOPE = 64
ROPE = 32
QK_DIM = NOPE + ROPE
V_DIM = 64
Q_LORA = 384
KV_LORA = 256
ROPE_THETA = 10000.0
SSM_WIDTH = 512
SSM_GROUP = 16
SSM_GROUPS = 32
SSM_STATE = 64
N_GROUPS = 4
PER_GROUP = 8
N_EXPERTS = 32
D_EXPERT = 256
EPS = 1e-6

LANES = 128
SUBLANES = 8
SSM_CHUNK = 16
N_PAIR = SSM_CHUNK // 2
SSM_QBLOCKS = SSM_WIDTH // LANES
GROUPS_PER_QBLOCK = LANES // SSM_GROUP
STATE_COLS = GROUPS_PER_QBLOCK * 2 * SSM_STATE
ROW_TILE = 512
ROW_UNROLL = 8
K1_SLABS = 1
K4_SLABS = 2
ROW_BLOCKS = D_MODEL // LANES
CHUNK_ROWS = SUBLANES
CHUNKS_PER_TILE = ROW_TILE // CHUNK_ROWS
VMEM_LIMIT = 56 * 1024 * 1024


def _local_rows(tm):
    return 2 * tm + N_EXPERTS * CHUNK_ROWS

C_Q = 0
C_KV = C_Q + Q_LORA
C_KR = C_KV + KV_LORA
C_U = C_KR + LANES
C_GATE = C_U + SSM_WIDTH
IN_COLS_PAD = C_GATE + 2 * D_MODEL


def _rms(x, g):
    return x * lax.rsqrt(jnp.mean(x * x, axis=-1, keepdims=True) + EPS) * g


def _dot(a, b):
    return jnp.dot(a, b, preferred_element_type=F32)


def _to_token_tiles(ref, x, n, base=0):
    for k in range(ROW_BLOCKS):
        ref[pl.ds(base * ROW_BLOCKS + k, n, stride=ROW_BLOCKS), :] = x[:, k * LANES:(k + 1) * LANES]


def _from_token_tiles(ref, n):
    return jnp.concatenate([ref[pl.ds(k, n, stride=ROW_BLOCKS), :] for k in range(ROW_BLOCKS)], axis=1)


def _k1_body(x_ref, gmix_ref, win_ref, gql_ref, wuq_ref, gkvl_ref, wukv_ref, gq_ref, gk_ref,
             cos_ref, sin_ref, q_ref, k_ref, v_ref, u_ref, gates_ref):
    vlane = lax.broadcasted_iota(jnp.int32, (1, HEADS * LANES), 1)
    ones_col = ((vlane % LANES) == V_DIM).astype(F32)
    ts = x_ref.shape[0] // K1_SLABS
    for s in range(K1_SLABS):
        rows = pl.ds(s * ts, ts)
        hb = _rms(x_ref[rows, :], gmix_ref[...]).astype(BF16)

        def proj(lo, hi):
            return _dot(hb, win_ref[:, lo:hi])

        lane = lax.broadcasted_iota(jnp.int32, (ts, LANES), 1)
        sin = sin_ref[rows, :]
        cos_t = jnp.where(lane < NOPE, 1.0, jnp.where(lane < QK_DIM, cos_ref[rows, :], 0.0))
        sin_m = jnp.where((lane >= NOPE) & (lane < NOPE + ROPE // 2), -sin, 0.0)
        sin_p = jnp.where((lane >= NOPE + ROPE // 2) & (lane < QK_DIM), sin, 0.0)

        def head_norm_rope(r, g):
            ms = jnp.sum(r * r, axis=-1, keepdims=True) * (1.0 / QK_DIM)
            rn = r * lax.rsqrt(ms + EPS) * g
            return (rn * cos_t + pltpu.roll(rn, LANES - ROPE // 2, 1) * sin_m
                    + pltpu.roll(rn, ROPE // 2, 1) * sin_p)

        qn = _rms(proj(C_Q, C_KV), gql_ref[...]).astype(BF16)
        qf = _dot(qn, wuq_ref[...])
        for h in range(HEADS):
            sl = slice(h * LANES, (h + 1) * LANES)
            q_ref[rows, sl] = head_norm_rope(qf[:, sl], gq_ref[:, sl]).astype(BF16)

        kvn = _rms(proj(C_KV, C_KR), gkvl_ref[...]).astype(BF16)
        kf = _dot(kvn, wukv_ref[...])
        kr = proj(C_KR, C_U)
        for h in range(HEADS):
            sl = slice(h * LANES, (h + 1) * LANES)
            k_ref[rows, sl] = head_norm_rope(kf[:, sl] + kr, gk_ref[:, sl]).astype(BF16)
        v_ref[rows, :] = (kf[:, HEADS * LANES:] + ones_col).astype(BF16)

        u_ref[rows, :] = proj(C_U, C_GATE)
        gates_ref[rows, :] = jax.nn.sigmoid(proj(C_GATE, IN_COLS_PAD)).astype(BF16)


def _k1(x, gmix, win, gql, wuq, gkvl, wukv, gq, gk, cos_t, sin_t, tm):
    t = x.shape[0]
    row = lambda w: pl.BlockSpec((tm, w), lambda i: (i, 0))
    full = lambda a: pl.BlockSpec(a.shape, lambda i: (0,) * a.ndim)
    return pl.pallas_call(
        _k1_body,
        grid=(t // tm,),
        in_specs=[row(D_MODEL), full(gmix), full(win), full(gql), full(wuq), full(gkvl), full(wukv),
                  full(gq), full(gk), row(LANES), row(LANES)],
        out_specs=[row(HEADS * LANES), row(HEADS * LANES), row(HEADS * LANES), row(SSM_WIDTH),
                   row(2 * D_MODEL)],
        out_shape=[jax.ShapeDtypeStruct((t, HEADS * LANES), BF16),
                   jax.ShapeDtypeStruct((t, HEADS * LANES), BF16),
                   jax.ShapeDtypeStruct((t, HEADS * LANES), BF16),
                   jax.ShapeDtypeStruct((t, SSM_WIDTH), F32),
                   jax.ShapeDtypeStruct((t, 2 * D_MODEL), BF16)],
        compiler_params=pltpu.CompilerParams(dimension_semantics=("arbitrary",),
                                             vmem_limit_bytes=VMEM_LIMIT),
        name="k1_inproj",
    )(x, gmix, win, gql, wuq, gkvl, wukv, gq, gk, cos_t, sin_t)


def _attn_body(q_ref, k_ref, v_ref, o_ref, *, tq, nsub, nq):
    i = pl.program_id(1)
    row_chunk = lax.broadcasted_iota(jnp.int32, (tq, tq), 0) // CHUNK
    col_chunk = lax.broadcasted_iota(jnp.int32, (tq, tq), 1) // CHUNK
    diag_mask = col_chunk <= row_chunk
    neg = jnp.float32(-1e30)

    def scores(rows, hh, n_full):
        sl = slice(hh * LANES, (hh + 1) * LANES)
        s = lax.dot_general(q_ref[0, rows, sl], k_ref[0, :n_full + tq, sl], (((1,), (1,)), ((), ())),
                            preferred_element_type=F32)
        s_diag = jnp.where(diag_mask, s[:, n_full:], neg)
        return jnp.concatenate([s[:, :n_full], s_diag], axis=1) if n_full else s_diag

    def probs(s):
        return jnp.exp2(s - jnp.max(s, axis=-1, keepdims=True)).astype(BF16)

    def values(p, hh, n_full):
        of = _dot(p, v_ref[0, :n_full + tq, hh * LANES:(hh + 1) * LANES])
        return of * (1.0 / of[:, V_DIM:V_DIM + 1])

    for ii in range(nq):
        @pl.when(i == ii)
        def _(ii=ii):
            lane = lax.broadcasted_iota(jnp.int32, (tq, LANES), 1)
            work = [(pl.ds(sub * tq, tq), (ii * nsub + sub) * tq) for sub in range(nsub)]
            ss = [[scores(rows, hh, n_full) for hh in range(HEADS)] for rows, n_full in work]
            ps = [[probs(s) for s in per_block] for per_block in ss]
            for (rows, n_full), per_block in zip(work, ps):
                os_ = [values(p, hh, n_full) for hh, p in enumerate(per_block)]
                for hp in range(HEADS // 2):
                    o_ref[0, rows, hp * LANES:(hp + 1) * LANES] = jnp.where(
                        lane < V_DIM, os_[2 * hp], pltpu.roll(os_[2 * hp + 1], V_DIM, 1)).astype(BF16)


def _attention(q, k, v, tq, nsub):
    b, l, _ = q.shape
    nq = l // (tq * nsub)
    return pl.pallas_call(
        functools.partial(_attn_body, tq=tq, nsub=nsub, nq=nq),
        grid=(b, nq),
        in_specs=[pl.BlockSpec((1, tq * nsub, HEADS * LANES), lambda bi, i: (bi, i, 0)),
                  pl.BlockSpec((1, l, HEADS * LANES), lambda bi, i: (bi, 0, 0)),
                  pl.BlockSpec((1, l, HEADS * LANES), lambda bi, i: (bi, 0, 0))],
        out_specs=pl.BlockSpec((1, tq * nsub, HEADS * V_DIM), lambda bi, i: (bi, i, 0)),
        out_shape=jax.ShapeDtypeStruct((b, l, HEADS * V_DIM), BF16),
        compiler_params=pltpu.CompilerParams(
            dimension_semantics=("arbitrary", "arbitrary"), vmem_limit_bytes=VMEM_LIMIT),
        name="attention",
    )(q, k, v)


def _cmul(a, b):
    return a[0] * b[0] - a[1] * b[1], a[0] * b[1] + a[1] * b[0]


def _ssm_tables(lam_re, lam_im, log_dt, b_re, b_im, c_re, c_im, d_skip):
    lc = SSM_CHUNK
    nq, gq = SSM_QBLOCKS, GROUPS_PER_QBLOCK
    lr, li = lam_re.astype(F32), lam_im.astype(F32)
    dt = jnp.exp(log_dt.astype(F32))[:, None]
    steps = jnp.arange(lc + 1, dtype=F32)[:, None, None]
    mag = jnp.exp(lr * dt * steps)
    pw = (mag * jnp.cos(li * dt * steps), mag * jnp.sin(li * dt * steps))
    num = (pw[0][1] - 1.0, pw[1][1])
    den = lr * lr + li * li
    ratio = ((num[0] * lr + num[1] * li) / den, (num[1] * lr - num[0] * li) / den)
    b_bar = _cmul((ratio[0][..., None], ratio[1][..., None]), (b_re.astype(F32), b_im.astype(F32)))
    c = (c_re.astype(F32), c_im.astype(F32))

    m = _cmul((pw[0][:lc, :, :, None], pw[1][:lc, :, :, None]), (b_bar[0][None], b_bar[1][None]))
    kern = jnp.einsum('gip,dgpj->dgij', c[0], m[0]) - jnp.einsum('gip,dgpj->dgij', c[1], m[1])
    kern = kern.at[0].add(jnp.eye(SSM_GROUP, dtype=F32)[None] * d_skip.astype(F32)[:, :, None])
    kpad = jnp.concatenate([jnp.zeros_like(kern[:1]), kern], axis=0)
    big_d = jnp.arange(N_PAIR)
    rows = []
    for s2 in range(2):
        cols = []
        for t2 in range(2):
            kd = kpad[2 * big_d + t2 - s2 + 1].reshape(N_PAIR, nq, gq, SSM_GROUP, SSM_GROUP)
            cols.append(jnp.transpose(kd, (1, 0, 4, 2, 3)))
        rows.append(jnp.stack(cols, axis=3))
    tab_t = jnp.stack(rows, axis=2).reshape(nq, N_PAIR, 2, SSM_GROUP, 2 * LANES)

    bt = (jnp.swapaxes(b_bar[0], 1, 2), jnp.swapaxes(b_bar[1], 1, 2))
    vin = _cmul((pw[0][:lc][::-1][:, :, None, :], pw[1][:lc][::-1][:, :, None, :]), (bt[0][None], bt[1][None]))
    vin = jnp.concatenate(vin, axis=-1).reshape(N_PAIR, 2, nq, gq, SSM_GROUP, 2 * SSM_STATE)
    tab_in = jnp.transpose(vin, (2, 0, 1, 4, 3, 5)).reshape(nq, N_PAIR, 2, SSM_GROUP, STATE_COLS)

    cw = _cmul((c[0][None], c[1][None]), (pw[0][1:lc + 1][:, :, None, :], pw[1][1:lc + 1][:, :, None, :]))
    cw = jnp.concatenate([cw[0], -cw[1]], axis=-1).reshape(N_PAIR, 2, nq, gq, SSM_GROUP, 2 * SSM_STATE)
    tab_out = jnp.transpose(cw, (2, 0, 1, 3, 4, 5)).reshape(nq, N_PAIR, 2 * LANES, 2 * SSM_STATE)

    ar = pw[0][lc].reshape(nq, gq, 1, SSM_STATE)
    ai = pw[1][lc].reshape(nq, gq, 1, SSM_STATE)
    a_same = jnp.concatenate([ar, ar], axis=2).reshape(nq, 1, STATE_COLS)
    a_swap = jnp.concatenate([-ai, ai], axis=2).reshape(nq, 1, STATE_COLS)
    return tab_t, tab_in, tab_out, jnp.concatenate([a_same, a_swap], axis=1)


def _ssm_body(u_ref, tt_ref, tin_ref, tout_ref, a_ref, y_ref, wt_ref, win_ref, wout_ref, st_ref, sw_ref,
              hp_ref, *, bb, nc):
    gq = GROUPS_PER_QBLOCK

    @pl.when(pl.program_id(1) == 0)
    def _():
        grp_t = (lax.broadcasted_iota(jnp.int32, (SSM_GROUP, 2 * LANES), 1) % LANES) // SSM_GROUP
        grp_in = lax.broadcasted_iota(jnp.int32, (SSM_GROUP, STATE_COLS), 1) // LANES
        grp_out = (lax.broadcasted_iota(jnp.int32, (LANES, 2 * LANES), 1) % LANES) // SSM_GROUP
        for d in range(N_PAIR):
            for s2 in range(2):
                piece_t = tt_ref[0, d, s2]
                piece_in = tin_ref[0, d, s2]
                for gl in range(gq):
                    rows = pl.ds(s2 * LANES + gl * SSM_GROUP, SSM_GROUP)
                    wt_ref[d, rows, :] = jnp.where(grp_t == gl, piece_t, 0.0).astype(BF16)
                    win_ref[d, rows, :] = jnp.where(grp_in == gl, piece_in, 0.0).astype(BF16)
            piece_out = tout_ref[0, d].T
            for gl in range(gq):
                wout_ref[d, pl.ds(gl * LANES, LANES), :] = jnp.where(grp_out == gl, piece_out, 0.0).astype(BF16)

    xs = []
    for s in range(N_PAIR):
        per_batch = []
        for b in range(bb):
            x0 = u_ref[b, pl.ds(2 * s, nc, stride=SSM_CHUNK), :]
            x1 = u_ref[b, pl.ds(2 * s + 1, nc, stride=SSM_CHUNK), :]
            per_batch.append(jnp.concatenate([x0, x1], axis=1))
        xs.append(jnp.concatenate(per_batch, axis=0).astype(BF16))

    st = _dot(xs[0], win_ref[0])
    for s in range(1, N_PAIR):
        st = st + _dot(xs[s], win_ref[s])
    st_ref[...] = st
    sw_ref[...] = jnp.concatenate(
        [pltpu.roll(st[:, g * LANES:(g + 1) * LANES], SSM_STATE, 1) for g in range(gq)], axis=1)

    a_same = a_ref[0, 0:1, :]
    a_swap = a_ref[0, 1:2, :]

    def step(c, carry):
        new = []
        for b in range(bb):
            h, hs = carry[b]
            row = b * nc + c
            hp_ref[pl.ds(row, 1), :] = h
            new.append((a_same * h + a_swap * hs + st_ref[pl.ds(row, 1), :],
                        a_same * hs - a_swap * h + sw_ref[pl.ds(row, 1), :]))
        return tuple(new)

    zero = jnp.zeros((1, STATE_COLS), F32)
    lax.fori_loop(0, nc, step, tuple((zero, zero) for _ in range(bb)))

    hp = hp_ref[...].astype(BF16)
    for t in range(N_PAIR):
        acc = _dot(hp, wout_ref[t])
        for s in range(t + 1):
            acc = acc + _dot(xs[s], wt_ref[t - s])
        yg = jax.nn.gelu(acc)
        for b in range(bb):
            rows = slice(b * nc, (b + 1) * nc)
            y_ref[b, pl.ds(2 * t, nc, stride=SSM_CHUNK), :] = yg[rows, :LANES]
            y_ref[b, pl.ds(2 * t + 1, nc, stride=SSM_CHUNK), :] = yg[rows, LANES:]


def _ssm(u, tab_t, tab_in, tab_out, a_chunk, bb):
    b, l, _ = u.shape
    nc = l // SSM_CHUNK
    wspec = lambda a: pl.BlockSpec((1,) + a.shape[1:], lambda q, bi: (q,) + (0,) * (a.ndim - 1))
    return pl.pallas_call(
        functools.partial(_ssm_body, bb=bb, nc=nc),
        grid=(SSM_QBLOCKS, b // bb),
        in_specs=[pl.BlockSpec((bb, l, LANES), lambda q, bi: (bi, 0, q)),
                  wspec(tab_t), wspec(tab_in), wspec(tab_out), wspec(a_chunk)],
        out_specs=pl.BlockSpec((bb, l, LANES), lambda q, bi: (bi, 0, q)),
        out_shape=jax.ShapeDtypeStruct((b, l, SSM_WIDTH), F32),
        scratch_shapes=[pltpu.VMEM((N_PAIR, 2 * LANES, 2 * LANES), BF16),
                        pltpu.VMEM((N_PAIR, 2 * LANES, STATE_COLS), BF16),
                        pltpu.VMEM((N_PAIR, STATE_COLS, 2 * LANES), BF16),
                        pltpu.VMEM((bb * nc, STATE_COLS), F32),
                        pltpu.VMEM((bb * nc, STATE_COLS), F32),
                        pltpu.VMEM((bb * nc, STATE_COLS), F32)],
        compiler_params=pltpu.CompilerParams(dimension_semantics=("arbitrary", "arbitrary"),
                                             vmem_limit_bytes=VMEM_LIMIT),
        name="ssm",
    )(u, tab_t, tab_in, tab_out, a_chunk)


def _k4_body(x_ref, o_ref, ys_ref, g_ref, wap_ref, wglu_ref, wout_ref, gffn_ref, wrh_ref, wrl_ref,
             xn_ref, xsl_ref, route_ref, routet_ref, cnt_ref, *, tm):
    ts = tm // K4_SLABS
    slabs = [pl.ds(s * ts, ts) for s in range(K4_SLABS)]

    logits_all = []
    h2b_all = []
    for s, rows in enumerate(slabs):
        y_attn = _dot(o_ref[rows, :], wap_ref[...])
        vg = _dot(ys_ref[rows, :].astype(BF16), wglu_ref[...])
        y_ssm = vg[:, :D_MODEL] * jax.nn.sigmoid(vg[:, D_MODEL:])
        g = g_ref[rows, :].astype(F32)
        merged = g[:, :D_MODEL] * y_attn + g[:, D_MODEL:] * y_ssm
        xn = x_ref[rows, :] + _dot(merged.astype(BF16), wout_ref[...])
        xn_ref[rows, :] = xn
        h2 = _rms(xn, gffn_ref[...])
        hh = h2.astype(BF16)
        hl = (h2 - hh.astype(F32)).astype(BF16)
        h2b_all.append(hh)
        logits_all.append(_dot(hh, wrh_ref[...]) + _dot(hh, wrl_ref[...]) + _dot(hl, wrh_ref[...]))

    lane = lax.broadcasted_iota(jnp.int32, (ts, LANES), 1).astype(F32)
    ninf = jnp.float32(-jnp.inf)
    big = jnp.float32(4 * LANES)
    tri = (lax.broadcasted_iota(jnp.int32, (ts, ts), 1)
           < lax.broadcasted_iota(jnp.int32, (ts, ts), 0)).astype(BF16)
    seen = jnp.zeros((1, LANES), F32)
    picks = []
    for s, rows in enumerate(slabs):
        logits = logits_all[s]

        def top(mask):
            val = jnp.max(jnp.where(mask, logits, ninf), axis=-1, keepdims=True)
            idx = jnp.min(jnp.where(mask & (logits == val), lane, big), axis=-1, keepdims=True)
            return val, idx

        gmask = lane < N_GROUPS
        gmax, gidx = top(gmask)
        g_w = 1.0 / jnp.sum(jnp.where(gmask, jnp.exp(logits - gmax), 0.0), axis=-1, keepdims=True)
        lo = N_GROUPS + PER_GROUP * gidx
        emask = (lane >= lo) & (lane < lo + PER_GROUP)
        l1, i1 = top(emask)
        l2, i2 = top(emask & (lane != i1))
        e21 = jnp.exp(l2 - l1)
        w1 = g_w / (1.0 + e21)
        w2 = g_w * e21 / (1.0 + e21)

        hot1 = lane == i1
        hot2 = lane == i2
        onehot = (hot1 | hot2).astype(BF16)
        picks.append((w1, w2, i1, i2, hot1, hot2, _dot(tri, onehot) + seen))
        seen = seen + jnp.sum(onehot.astype(F32), axis=0, keepdims=True)
    cnt_ref[0] = seen

    padded = jnp.floor((seen + (CHUNK_ROWS - 1)) * (1.0 / CHUNK_ROWS)) * CHUNK_ROWS
    earlier = (lax.broadcasted_iota(jnp.int32, (LANES, LANES), 0)
               < lax.broadcasted_iota(jnp.int32, (LANES, LANES), 1)).astype(BF16)
    start = _dot(jnp.broadcast_to(padded, (SUBLANES, LANES)).astype(BF16), earlier)[0:1, :]

    local_rows = []
    for s, rows in enumerate(slabs):
        w1, w2, i1, i2, hot1, hot2, before = picks[s]
        r1 = jnp.sum(jnp.where(hot1, before + start, 0.0), axis=-1, keepdims=True)
        r2 = jnp.sum(jnp.where(hot2, before + start, 0.0), axis=-1, keepdims=True)
        cols = (w1, w2, i1 - N_GROUPS, i2 - N_GROUPS, r1, r2)
        out = jnp.zeros((ts, LANES), F32)
        for n, col in enumerate(cols):
            out = jnp.where(lane == n, col, out)
        route_ref[rows, :] = out
        out_t = out.T[:SUBLANES, :]
        routet_ref[:, rows] = out_t
        local_rows.append(out_t[4:6, :])

    lr = jnp.concatenate(local_rows, axis=1)
    n_local = xsl_ref.shape[0]
    row_id = lax.broadcasted_iota(jnp.int32, (n_local, tm), 0).astype(F32)
    select = ((row_id == lr[0:1, :]) | (row_id == lr[1:2, :])).astype(BF16)
    xsl_ref[...] = _dot(select, jnp.concatenate(h2b_all, axis=0))


def _k4(x, o, ys, gates, wap, wglu, wout, gffn, wrh, wrl, tm):
    t = x.shape[0]
    row = lambda w: pl.BlockSpec((tm, w), lambda i: (i, 0))
    full = lambda a: pl.BlockSpec(a.shape, lambda i: (0,) * a.ndim)
    return pl.pallas_call(
        functools.partial(_k4_body, tm=tm),
        grid=(t // tm,),
        in_specs=[row(D_MODEL), row(HEADS * V_DIM), row(SSM_WIDTH), row(2 * D_MODEL),
                  full(wap), full(wglu), full(wout), full(gffn), full(wrh), full(wrl)],
        out_specs=[row(D_MODEL), pl.BlockSpec((_local_rows(tm), D_MODEL), lambda i: (i, 0)), row(LANES),
                   pl.BlockSpec((SUBLANES, tm), lambda i: (0, i)),
                   pl.BlockSpec((1, 1, LANES), lambda i: (i, 0, 0))],
        out_shape=[jax.ShapeDtypeStruct((t, D_MODEL), F32),
                   jax.ShapeDtypeStruct((t // tm * _local_rows(tm), D_MODEL), F32),
                   jax.ShapeDtypeStruct((t, LANES), F32), jax.ShapeDtypeStruct((SUBLANES, t), F32),
                   jax.ShapeDtypeStruct((t // tm, 1, LANES), F32)],
        compiler_params=pltpu.CompilerParams(dimension_semantics=("arbitrary",),
                                             vmem_limit_bytes=VMEM_LIMIT),
        name="k4_merge_router",
    )(x, o, ys, gates, wap, wglu, wout, gffn, wrh, wrl)


def _row_copy(src, dst, sem):
    return pltpu.make_async_copy(src, dst, sem)


def _token_tile(ref, r):
    return ref.at[pl.ds(pl.multiple_of(r * ROW_BLOCKS, ROW_BLOCKS), ROW_BLOCKS)]


def _expert_body(te_ref, tv_ref, cs_ref, xsl_ref, wg_ref, wu_ref, wd_ref, ys_ref,
                 wgb_ref, wub_ref, wdb_ref, xbuf_ref, sem, *, n_steps):
    i = pl.program_id(0)
    slot = lax.rem(i, 2)

    def fetch(tile, into):
        def start(c2, _):
            for kk in range(2):
                c = 2 * c2 + kk
                chunk = cs_ref[tile * CHUNKS_PER_TILE + c]
                src = xsl_ref.at[pl.ds(pl.multiple_of(chunk * CHUNK_ROWS, CHUNK_ROWS), CHUNK_ROWS)]
                dst = xbuf_ref.at[into, pl.ds(pl.multiple_of(c * CHUNK_ROWS, CHUNK_ROWS), CHUNK_ROWS)]
                _row_copy(src, dst, sem.at[into]).start(priority=kk)
            return 0
        lax.fori_loop(0, CHUNKS_PER_TILE // 2, start, 0, unroll=ROW_UNROLL // 2)

    @pl.when(i == 0)
    def _():
        fetch(0, 0)

    @pl.when((i + 1 < n_steps) & (tv_ref[jnp.minimum(i + 1, n_steps - 1)] == 1))
    def _():
        fetch(i + 1, 1 - slot)

    prev = te_ref[jnp.maximum(i - 1, 0)]

    @pl.when((i == 0) | (te_ref[i] != prev))
    def _():
        wgb_ref[...] = wg_ref[0, 0].astype(BF16)
        wub_ref[...] = wu_ref[0, 0].astype(BF16)
        wdb_ref[...] = wd_ref[0, 0].astype(BF16)

    @pl.when(tv_ref[i] == 1)
    def _():
        _row_copy(xsl_ref.at[pl.ds(0, ROW_TILE)], xbuf_ref.at[slot], sem.at[slot]).wait()
        xb = xbuf_ref[slot].astype(BF16)
        hidden = jax.nn.silu(_dot(xb, wgb_ref[...])) * _dot(xb, wub_ref[...])
        _to_token_tiles(ys_ref, _dot(hidden.astype(BF16), wdb_ref[...]), ROW_TILE)

    @pl.when(tv_ref[i] == 0)
    def _():
        ys_ref[...] = jnp.zeros_like(ys_ref)


def _experts(tile_expert, tile_valid, chunk_src, xsl, wg, wu, wd, layer):
    n_steps = tile_expert.shape[0]
    tile_rows = ROW_TILE * ROW_BLOCKS
    grid_spec = pltpu.PrefetchScalarGridSpec(
        num_scalar_prefetch=3,
        grid=(n_steps,),
        in_specs=[pl.BlockSpec(memory_space=pl.ANY),
                  pl.BlockSpec((1, 1, D_MODEL, D_EXPERT), lambda i, te, tv, cs: (layer, te[i], 0, 0)),
                  pl.BlockSpec((1, 1, D_MODEL, D_EXPERT), lambda i, te, tv, cs: (layer, te[i], 0, 0)),
                  pl.BlockSpec((1, 1, D_EXPERT, D_MODEL), lambda i, te, tv, cs: (layer, te[i], 0, 0))],
        out_specs=pl.BlockSpec((tile_rows, LANES), lambda i, te, tv, cs: (i, 0)),
        scratch_shapes=[pltpu.VMEM((D_MODEL, D_EXPERT), BF16), pltpu.VMEM((D_MODEL, D_EXPERT), BF16),
                        pltpu.VMEM((D_EXPERT, D_MODEL), BF16), pltpu.VMEM((2, ROW_TILE, D_MODEL), F32),
                        pltpu.SemaphoreType.DMA((2,))],
    )
    return pl.pallas_call(
        functools.partial(_expert_body, n_steps=n_steps),
        grid_spec=grid_spec,
        out_shape=jax.ShapeDtypeStruct((n_steps * tile_rows, LANES), F32),
        compiler_params=pltpu.CompilerParams(dimension_semantics=("arbitrary",),
                                             vmem_limit_bytes=VMEM_LIMIT),
        name="moe_experts",
    )(tile_expert, tile_valid, chunk_src, xsl, wg, wu, wd)


def _combine_body(pos_ref, x_ref, route_ref, ys_ref, out_ref, buf_ref, sem, *, tc):
    def start(r, _):
        for kk in range(2):
            _row_copy(_token_tile(ys_ref, pos_ref[0, kk, r]), _token_tile(buf_ref.at[kk], r),
                      sem).start(priority=kk)
        return 0

    lax.fori_loop(0, tc, start, 0, unroll=ROW_UNROLL)
    for kk in range(2):
        _row_copy(ys_ref.at[pl.ds(0, tc * ROW_BLOCKS)], buf_ref.at[kk], sem).wait()
    route = route_ref[...]
    out_ref[...] = (x_ref[...] + route[:, 0:1] * _from_token_tiles(buf_ref.at[0], tc)
                    + route[:, 1:2] * _from_token_tiles(buf_ref.at[1], tc))


def _combine(pos, x, route, ys, tc):
    t = x.shape[0]
    return pl.pallas_call(
        functools.partial(_combine_body, tc=tc),
        grid=(t // tc,),
        in_specs=[pl.BlockSpec((1, 2, tc), lambda i: (i, 0, 0), memory_space=pltpu.SMEM),
                  pl.BlockSpec((tc, D_MODEL), lambda i: (i, 0)),
                  pl.BlockSpec((tc, LANES), lambda i: (i, 0)),
                  pl.BlockSpec(memory_space=pl.ANY)],
        out_specs=pl.BlockSpec((tc, D_MODEL), lambda i: (i, 0)),
        out_shape=jax.ShapeDtypeStruct((t, D_MODEL), F32),
        scratch_shapes=[pltpu.VMEM((2, tc * ROW_BLOCKS, LANES), F32), pltpu.SemaphoreType.DMA(())],
        compiler_params=pltpu.CompilerParams(dimension_semantics=("arbitrary",)),
        name="moe_combine",
    )(pos, x, route, ys)


def _expert_tiles(t, tm):
    return (2 * t + (t // tm) * N_EXPERTS * (CHUNK_ROWS - 1)) // ROW_TILE + N_EXPERTS


def _routing_plan(route_t, counts, tm):
    i32 = jnp.int32
    t = route_t.shape[1]
    n_tok_tiles = t // tm
    local_chunks = _local_rows(tm) // CHUNK_ROWS
    n_tiles = _expert_tiles(t, tm)
    cnt = counts[:, 0, N_GROUPS:N_GROUPS + N_EXPERTS].astype(i32)
    chunks = (cnt + CHUNK_ROWS - 1) // CHUNK_ROWS
    first_local = jnp.cumsum(chunks, axis=1) - chunks
    upto = jnp.cumsum(chunks, axis=0)
    before = upto - chunks
    n_chunks = upto[-1]
    n_exp_tiles = (n_chunks + CHUNKS_PER_TILE - 1) // CHUNKS_PER_TILE
    tile_end = jnp.cumsum(n_exp_tiles)
    tile_start = tile_end - n_exp_tiles
    used = tile_end[-1]
    tiles = jnp.arange(n_tiles, dtype=i32)
    valid = tiles < used
    last = jnp.maximum(used - 1, 0)
    expert_of = jnp.sum((tile_end[None, :] <= tiles[:, None]).astype(i32), axis=1)
    tile_expert = jnp.where(valid, expert_of, jnp.sum(jnp.where(tiles == last, expert_of, 0)))

    sel = tile_expert[:, None] == jnp.arange(N_EXPERTS, dtype=i32)[None, :]
    per_tile = lambda table: jnp.sum(jnp.where(sel[:, :, None], table.T[None], 0), axis=1)
    per_tile1 = lambda vec: jnp.sum(jnp.where(sel, vec[None, :], 0), axis=1)
    upto_g, before_g, first_g = per_tile(upto), per_tile(before), per_tile(first_local)
    k = ((tiles - per_tile1(tile_start))[:, None] * CHUNKS_PER_TILE
         + jnp.arange(CHUNKS_PER_TILE, dtype=i32)[None, :])
    k_valid = valid[:, None] & (k < per_tile1(n_chunks)[:, None])
    src_tile = jnp.minimum(jnp.sum((upto_g[:, None, :] <= k[:, :, None]).astype(i32), axis=2), n_tok_tiles - 1)
    at = src_tile[:, :, None] == jnp.arange(n_tok_tiles, dtype=i32)[None, None, :]
    before_at = jnp.sum(jnp.where(at, before_g[:, None, :], 0), axis=2)
    first_at = jnp.sum(jnp.where(at, first_g[:, None, :], 0), axis=2)
    chunk = src_tile * local_chunks + first_at + (k - before_at)
    chunk_src = jnp.where(k_valid, chunk, local_chunks - 1).reshape(-1).astype(i32)

    base = tile_start[None, :] * ROW_TILE + CHUNK_ROWS * (before - first_local)
    base_t = jnp.repeat(base.T, tm, axis=1)
    experts = jnp.arange(N_EXPERTS, dtype=i32)[:, None]

    def position(e_row, local_row):
        e = e_row.astype(i32)[None, :]
        return jnp.sum(jnp.where(e == experts, base_t, 0), axis=0) + local_row.astype(i32)

    pos = jnp.stack([position(route_t[2], route_t[4]).reshape(n_tok_tiles, tm),
                     position(route_t[3], route_t[5]).reshape(n_tok_tiles, tm)], axis=1)
    return pos, tile_expert, valid.astype(i32), chunk_src


def _rope_tables(positions):
    half = ROPE // 2
    inv_freq = ROPE_THETA ** (-jnp.arange(half, dtype=F32) / half)
    ang = inv_freq[:, None] * positions.astype(F32).reshape(1, -1)
    spread = lambda a: jnp.tile(a.T, (1, LANES // half))
    return spread(jnp.cos(ang)), spread(jnp.sin(ang))


def _head_slots(w, width):
    k = w.shape[0]
    w = w.reshape(k, HEADS, width)
    return jnp.pad(w, ((0, 0), (0, 0), (0, LANES - width))).reshape(k, HEADS * LANES)


def _layer_weights(l, w_in, w_uq, w_ukv, q_head_g, k_head_g, w_router_group, w_router_expert):
    wi = w_in[l]
    kr = jnp.pad(wi[:, C_KR:C_KR + ROPE], ((0, 0), (NOPE, LANES - QK_DIM)))
    win = jnp.concatenate([wi[:, :C_KR], kr, wi[:, C_KR + ROPE:]], axis=1).astype(BF16)
    wuq = _head_slots(w_uq[l], QK_DIM).astype(BF16)
    kv = w_ukv[l].reshape(KV_LORA, HEADS, NOPE + V_DIM)
    wuk = _head_slots(kv[:, :, :NOPE].reshape(KV_LORA, HEADS * NOPE), NOPE)
    wuv = _head_slots(kv[:, :, NOPE:].reshape(KV_LORA, HEADS * V_DIM), V_DIM)
    wukv = jnp.concatenate([wuk, wuv], axis=1).astype(BF16)
    pad_g = lambda g: jnp.tile(jnp.pad(g.astype(F32), (0, LANES - QK_DIM)), HEADS)[None, :]
    gq = pad_g(q_head_g[l]) * (QK_DIM ** -0.5 * math.log2(math.e))
    gk = pad_g(k_head_g[l])
    wr = jnp.concatenate([w_router_group[l], w_router_expert[l]], axis=1).astype(F32)
    wr = jnp.pad(wr, ((0, 0), (0, LANES - wr.shape[1])))
    wrh = wr.astype(BF16)
    wrl = (wr - wrh.astype(F32)).astype(BF16)
    return win, wuq, wukv, gq, gk, wrh, wrl


def kernel(x, positions, norm_mix_g, w_in, q_lora_g, w_uq, kv_lora_g, w_ukv, q_head_g, k_head_g, w_attn_proj, lam_re, lam_im, log_dt, b_re, b_im, c_re, c_im, d_skip, w_glu, w_out, norm_ffn_g, w_router_group, w_router_expert, w_exp_gate, w_exp_up, w_exp_down):
    bsz, seq, _ = x.shape
    t = bsz * seq
    depth = w_in.shape[0]
    tm = 512 if t % 512 == 0 else 256
    tq = 256
    nsub = 2 if seq % (2 * tq) == 0 else 1
    bb = 2 if bsz % 2 == 0 else 1
    row1 = lambda g: g.astype(F32)[None, :]

    cos_t, sin_t = _rope_tables(positions)
    xf = x.reshape(t, D_MODEL).astype(F32)
    for l in range(depth):
        win, wuq, wukv, gq, gk, wrh, wrl = _layer_weights(
            l, w_in, w_uq, w_ukv, q_head_g, k_head_g, w_router_group, w_router_expert)
        q, k, v, u, gates = _k1(xf, row1(norm_mix_g[l]), win, row1(q_lora_g[l]), wuq,
                                row1(kv_lora_g[l]), wukv, gq, gk, cos_t, sin_t, tm)
        o = _attention(q.reshape(bsz, seq, -1), k.reshape(bsz, seq, -1), v.reshape(bsz, seq, -1), tq, nsub)
        tables = _ssm_tables(lam_re[l], lam_im[l], log_dt[l], b_re[l], b_im[l], c_re[l], c_im[l], d_skip[l])
        ys = _ssm(u.reshape(bsz, seq, SSM_WIDTH), *tables, bb)
        xn, xsl, route, route_t, counts = _k4(xf, o.reshape(t, -1), ys.reshape(t, SSM_WIDTH), gates,
                                              w_attn_proj[l].astype(BF16), w_glu[l].astype(BF16),
                                              w_out[l].astype(BF16), row1(norm_ffn_g[l]), wrh, wrl, tm)
        pos, tile_expert, tile_valid, chunk_src = _routing_plan(route_t, counts, tm)
        ye = _experts(tile_expert, tile_valid, chunk_src, xsl, w_exp_gate, w_exp_up, w_exp_down, l)
        xf = _combine(pos, xn, route, ye, tm)
    return xf.reshape(bsz, seq, D_MODEL).astype(x.dtype)
```

```python
import functools
import math

import jax
import jax.numpy as jnp
from jax import lax
from jax.experimental import pallas as pl
from jax.experimental.pallas import tpu as pltpu

F32 = jnp.float32
BF16 = jnp.bfloat16

D_MODEL = 1024
CHUNK = 64
HEADS = 8
NOPE = 64
ROPE = 32
QK_DIM = NOPE + ROPE
V_DIM = 64
Q_LORA = 384
KV_LORA = 256
ROPE_THETA = 10000.0
SSM_WIDTH = 512
SSM_GROUP = 16
SSM_GROUPS = 32
SSM_STATE = 64
N_GROUPS = 4
PER_GROUP = 8
N_EXPERTS = 32
D_EXPERT = 256
EPS = 1e-6

LANES = 128
SUBLANES = 8
SSM_CHUNK = 16
N_PAIR = SSM_CHUNK // 2
SSM_QBLOCKS = SSM_WIDTH // LANES
GROUPS_PER_QBLOCK = LANES // SSM_GROUP
STATE_COLS = GROUPS_PER_QBLOCK * 2 * SSM_STATE
ROW_TILE = 512
ROW_UNROLL = 8
K1_SLABS = 1
K4_SLABS = 2
ROW_BLOCKS = D_MODEL // LANES
CHUNK_ROWS = SUBLANES
CHUNKS_PER_TILE = ROW_TILE // CHUNK_ROWS
VMEM_LIMIT = 56 * 1024 * 1024


def _local_rows(tm):
    return 2 * tm + N_EXPERTS * CHUNK_ROWS

C_Q = 0
C_KV = C_Q + Q_LORA
C_KR = C_KV + KV_LORA
C_U = C_KR + LANES
C_GATE = C_U + SSM_WIDTH
IN_COLS_PAD = C_GATE + 2 * D_MODEL


def _rms(x, g):
    return x * lax.rsqrt(jnp.mean(x * x, axis=-1, keepdims=True) + EPS) * g


def _dot(a, b):
    return jnp.dot(a, b, preferred_element_type=F32)


def _to_token_tiles(ref, x, n, base=0):
    for k in range(ROW_BLOCKS):
        ref[pl.ds(base * ROW_BLOCKS + k, n, stride=ROW_BLOCKS), :] = x[:, k * LANES:(k + 1) * LANES]


def _from_token_tiles(ref, n):
    return jnp.concatenate([ref[pl.ds(k, n, stride=ROW_BLOCKS), :] for k in range(ROW_BLOCKS)], axis=1)


def _k1_body(x_ref, gmix_ref, win_ref, gql_ref, wuq_ref, gkvl_ref, wukv_ref, gq_ref, gk_ref,
             cos_ref, sin_ref, q_ref, k_ref, v_ref, u_ref, gates_ref):
    vlane = lax.broadcasted_iota(jnp.int32, (1, HEADS * LANES), 1)
    ones_col = ((vlane % LANES) == V_DIM).astype(F32)
    ts = x_ref.shape[0] // K1_SLABS
    for s in range(K1_SLABS):
        rows = pl.ds(s * ts, ts)
        hb = _rms(x_ref[rows, :], gmix_ref[...]).astype(BF16)

        def proj(lo, hi):
            return _dot(hb, win_ref[:, lo:hi])

        lane = lax.broadcasted_iota(jnp.int32, (ts, LANES), 1)
        sin = sin_ref[rows, :]
        cos_t = jnp.where(lane < NOPE, 1.0, jnp.where(lane < QK_DIM, cos_ref[rows, :], 0.0))
        sin_m = jnp.where((lane >= NOPE) & (lane < NOPE + ROPE // 2), -sin, 0.0)
        sin_p = jnp.where((lane >= NOPE + ROPE // 2) & (lane < QK_DIM), sin, 0.0)

        def head_norm_rope(r, g):
            ms = jnp.sum(r * r, axis=-1, keepdims=True) * (1.0 / QK_DIM)
            rn = r * lax.rsqrt(ms + EPS) * g
            return (rn * cos_t + pltpu.roll(rn, LANES - ROPE // 2, 1) * sin_m
                    + pltpu.roll(rn, ROPE // 2, 1) * sin_p)

        qn = _rms(proj(C_Q, C_KV), gql_ref[...]).astype(BF16)
        qf = _dot(qn, wuq_ref[...])
        for h in range(HEADS):
            sl = slice(h * LANES, (h + 1) * LANES)
            q_ref[rows, sl] = head_norm_rope(qf[:, sl], gq_ref[:, sl]).astype(BF16)

        kvn = _rms(proj(C_KV, C_KR), gkvl_ref[...]).astype(BF16)
        kf = _dot(kvn, wukv_ref[...])
        kr = proj(C_KR, C_U)
        for h in range(HEADS):
            sl = slice(h * LANES, (h + 1) * LANES)
            k_ref[rows, sl] = head_norm_rope(kf[:, sl] + kr, gk_ref[:, sl]).astype(BF16)
        v_ref[rows, :] = (kf[:, HEADS * LANES:] + ones_col).astype(BF16)

        u_ref[rows, :] = proj(C_U, C_GATE)
        gates_ref[rows, :] = jax.nn.sigmoid(proj(C_GATE, IN_COLS_PAD)).astype(BF16)


def _k1(x, gmix, win, gql, wuq, gkvl, wukv, gq, gk, cos_t, sin_t, tm):
    t = x.shape[0]
    row = lambda w: pl.BlockSpec((tm, w), lambda i: (i, 0))
    full = lambda a: pl.BlockSpec(a.shape, lambda i: (0,) * a.ndim)
    return pl.pallas_call(
        _k1_body,
        grid=(t // tm,),
        in_specs=[row(D_MODEL), full(gmix), full(win), full(gql), full(wuq), full(gkvl), full(wukv),
                  full(gq), full(gk), row(LANES), row(LANES)],
        out_specs=[row(HEADS * LANES), row(HEADS * LANES), row(HEADS * LANES), row(SSM_WIDTH),
                   row(2 * D_MODEL)],
        out_shape=[jax.ShapeDtypeStruct((t, HEADS * LANES), BF16),
                   jax.ShapeDtypeStruct((t, HEADS * LANES), BF16),
                   jax.ShapeDtypeStruct((t, HEADS * LANES), BF16),
                   jax.ShapeDtypeStruct((t, SSM_WIDTH), F32),
                   jax.ShapeDtypeStruct((t, 2 * D_MODEL), BF16)],
        compiler_params=pltpu.CompilerParams(dimension_semantics=("arbitrary",),
                                             vmem_limit_bytes=VMEM_LIMIT),
        name="k1_inproj",
    )(x, gmix, win, gql, wuq, gkvl, wukv, gq, gk, cos_t, sin_t)


def _attn_body(q_ref, k_ref, v_ref, o_ref, *, tq, nsub, nq):
    i = pl.program_id(1)
    row_chunk = lax.broadcasted_iota(jnp.int32, (tq, tq), 0) // CHUNK
    col_chunk = lax.broadcasted_iota(jnp.int32, (tq, tq), 1) // CHUNK
    diag_mask = col_chunk <= row_chunk
    neg = jnp.float32(-1e30)

    def scores(rows, hh, n_full):
        sl = slice(hh * LANES, (hh + 1) * LANES)
        s = lax.dot_general(q_ref[0, rows, sl], k_ref[0, :n_full + tq, sl], (((1,), (1,)), ((), ())),
                            preferred_element_type=F32)
        s_diag = jnp.where(diag_mask, s[:, n_full:], neg)
        return jnp.concatenate([s[:, :n_full], s_diag], axis=1) if n_full else s_diag

    def probs(s):
        return jnp.exp2(s - jnp.max(s, axis=-1, keepdims=True)).astype(BF16)

    def values(p, hh, n_full):
        of = _dot(p, v_ref[0, :n_full + tq, hh * LANES:(hh + 1) * LANES])
        return of * (1.0 / of[:, V_DIM:V_DIM + 1])

    for ii in range(nq):
        @pl.when(i == ii)
        def _(ii=ii):
            lane = lax.broadcasted_iota(jnp.int32, (tq, LANES), 1)
            work = [(pl.ds(sub * tq, tq), (ii * nsub + sub) * tq) for sub in range(nsub)]
            ss = [[scores(rows, hh, n_full) for hh in range(HEADS)] for rows, n_full in work]
            ps = [[probs(s) for s in per_block] for per_block in ss]
            for (rows, n_full), per_block in zip(work, ps):
                os_ = [values(p, hh, n_full) for hh, p in enumerate(per_block)]
                for hp in range(HEADS // 2):
                    o_ref[0, rows, hp * LANES:(hp + 1) * LANES] = jnp.where(
                        lane < V_DIM, os_[2 * hp], pltpu.roll(os_[2 * hp + 1], V_DIM, 1)).astype(BF16)


def _attention(q, k, v, tq, nsub):
    b, l, _ = q.shape
    nq = l // (tq * nsub)
    return pl.pallas_call(
        functools.partial(_attn_body, tq=tq, nsub=nsub, nq=nq),
        grid=(b, nq),
        in_specs=[pl.BlockSpec((1, tq * nsub, HEADS * LANES), lambda bi, i: (bi, i, 0)),
                  pl.BlockSpec((1, l, HEADS * LANES), lambda bi, i: (bi, 0, 0)),
                  pl.BlockSpec((1, l, HEADS * LANES), lambda bi, i: (bi, 0, 0))],
        out_specs=pl.BlockSpec((1, tq * nsub, HEADS * V_DIM), lambda bi, i: (bi, i, 0)),
        out_shape=jax.ShapeDtypeStruct((b, l, HEADS * V_DIM), BF16),
        compiler_params=pltpu.CompilerParams(
            dimension_semantics=("arbitrary", "arbitrary"), vmem_limit_bytes=VMEM_LIMIT),
        name="attention",
    )(q, k, v)


def _cmul(a, b):
    return a[0] * b[0] - a[1] * b[1], a[0] * b[1] + a[1] * b[0]


def _ssm_tables(lam_re, lam_im, log_dt, b_re, b_im, c_re, c_im, d_skip):
    lc = SSM_CHUNK
    nq, gq = SSM_QBLOCKS, GROUPS_PER_QBLOCK
    lr, li = lam_re.astype(F32), lam_im.astype(F32)
    dt = jnp.exp(log_dt.astype(F32))[:, None]
    steps = jnp.arange(lc + 1, dtype=F32)[:, None, None]
    mag = jnp.exp(lr * dt * steps)
    pw = (mag * jnp.cos(li * dt * steps), mag * jnp.sin(li * dt * steps))
    num = (pw[0][1] - 1.0, pw[1][1])
    den = lr * lr + li * li
    ratio = ((num[0] * lr + num[1] * li) / den, (num[1] * lr - num[0] * li) / den)
    b_bar = _cmul((ratio[0][..., None], ratio[1][..., None]), (b_re.astype(F32), b_im.astype(F32)))
    c = (c_re.astype(F32), c_im.astype(F32))

    m = _cmul((pw[0][:lc, :, :, None], pw[1][:lc, :, :, None]), (b_bar[0][None], b_bar[1][None]))
    kern = jnp.einsum('gip,dgpj->dgij', c[0], m[0]) - jnp.einsum('gip,dgpj->dgij', c[1], m[1])
    kern = kern.at[0].add(jnp.eye(SSM_GROUP, dtype=F32)[None] * d_skip.astype(F32)[:, :, None])
    kpad = jnp.concatenate([jnp.zeros_like(kern[:1]), kern], axis=0)
    big_d = jnp.arange(N_PAIR)
    rows = []
    for s2 in range(2):
        cols = []
        for t2 in range(2):
            kd = kpad[2 * big_d + t2 - s2 + 1].reshape(N_PAIR, nq, gq, SSM_GROUP, SSM_GROUP)
            cols.append(jnp.transpose(kd, (1, 0, 4, 2, 3)))
        rows.append(jnp.stack(cols, axis=3))
    tab_t = jnp.stack(rows, axis=2).reshape(nq, N_PAIR, 2, SSM_GROUP, 2 * LANES)

    bt = (jnp.swapaxes(b_bar[0], 1, 2), jnp.swapaxes(b_bar[1], 1, 2))
    vin = _cmul((pw[0][:lc][::-1][:, :, None, :], pw[1][:lc][::-1][:, :, None, :]), (bt[0][None], bt[1][None]))
    vin = jnp.concatenate(vin, axis=-1).reshape(N_PAIR, 2, nq, gq, SSM_GROUP, 2 * SSM_STATE)
    tab_in = jnp.transpose(vin, (2, 0, 1, 4, 3, 5)).reshape(nq, N_PAIR, 2, SSM_GROUP, STATE_COLS)

    cw = _cmul((c[0][None], c[1][None]), (pw[0][1:lc + 1][:, :, None, :], pw[1][1:lc + 1][:, :, None, :]))
    cw = jnp.concatenate([cw[0], -cw[1]], axis=-1).reshape(N_PAIR, 2, nq, gq, SSM_GROUP, 2 * SSM_STATE)
    tab_out = jnp.transpose(cw, (2, 0, 1, 3, 4, 5)).reshape(nq, N_PAIR, 2 * LANES, 2 * SSM_STATE)

    ar = pw[0][lc].reshape(nq, gq, 1, SSM_STATE)
    ai = pw[1][lc].reshape(nq, gq, 1, SSM_STATE)
    a_same = jnp.concatenate([ar, ar], axis=2).reshape(nq, 1, STATE_COLS)
    a_swap = jnp.concatenate([-ai, ai], axis=2).reshape(nq, 1, STATE_COLS)
    return tab_t, tab_in, tab_out, jnp.concatenate([a_same, a_swap], axis=1)


def _ssm_body(u_ref, tt_ref, tin_ref, tout_ref, a_ref, y_ref, wt_ref, win_ref, wout_ref, st_ref, sw_ref,
              hp_ref, *, bb, nc):
    gq = GROUPS_PER_QBLOCK

    @pl.when(pl.program_id(1) == 0)
    def _():
        grp_t = (lax.broadcasted_iota(jnp.int32, (SSM_GROUP, 2 * LANES), 1) % LANES) // SSM_GROUP
        grp_in = lax.broadcasted_iota(jnp.int32, (SSM_GROUP, STATE_COLS), 1) // LANES
        grp_out = (lax.broadcasted_iota(jnp.int32, (LANES, 2 * LANES), 1) % LANES) // SSM_GROUP
        for d in range(N_PAIR):
            for s2 in range(2):
                piece_t = tt_ref[0, d, s2]
                piece_in = tin_ref[0, d, s2]
                for gl in range(gq):
                    rows = pl.ds(s2 * LANES + gl * SSM_GROUP, SSM_GROUP)
                    wt_ref[d, rows, :] = jnp.where(grp_t == gl, piece_t, 0.0).astype(BF16)
                    win_ref[d, rows, :] = jnp.where(grp_in == gl, piece_in, 0.0).astype(BF16)
            piece_out = tout_ref[0, d].T
            for gl in range(gq):
                wout_ref[d, pl.ds(gl * LANES, LANES), :] = jnp.where(grp_out == gl, piece_out, 0.0).astype(BF16)

    xs = []
    for s in range(N_PAIR):
        per_batch = []
        for b in range(bb):
            x0 = u_ref[b, pl.ds(2 * s, nc, stride=SSM_CHUNK), :]
            x1 = u_ref[b, pl.ds(2 * s + 1, nc, stride=SSM_CHUNK), :]
            per_batch.append(jnp.concatenate([x0, x1], axis=1))
        xs.append(jnp.concatenate(per_batch, axis=0).astype(BF16))

    st = _dot(xs[0], win_ref[0])
    for s in range(1, N_PAIR):
        st = st + _dot(xs[s], win_ref[s])
    st_ref[...] = st
    sw_ref[...] = jnp.concatenate(
        [pltpu.roll(st[:, g * LANES:(g + 1) * LANES], SSM_STATE, 1) for g in range(gq)], axis=1)

    a_same = a_ref[0, 0:1, :]
    a_swap = a_ref[0, 1:2, :]

    def step(c, carry):
        new = []
        for b in range(bb):
            h, hs = carry[b]
            row = b * nc + c
            hp_ref[pl.ds(row, 1), :] = h
            new.append((a_same * h + a_swap * hs + st_ref[pl.ds(row, 1), :],
                        a_same * hs - a_swap * h + sw_ref[pl.ds(row, 1), :]))
        return tuple(new)

    zero = jnp.zeros((1, STATE_COLS), F32)
    lax.fori_loop(0, nc, step, tuple((zero, zero) for _ in range(bb)))

    hp = hp_ref[...].astype(BF16)
    for t in range(N_PAIR):
        acc = _dot(hp, wout_ref[t])
        for s in range(t + 1):
            acc = acc + _dot(xs[s], wt_ref[t - s])
        yg = jax.nn.gelu(acc)
        for b in range(bb):
            rows = slice(b * nc, (b + 1) * nc)
            y_ref[b, pl.ds(2 * t, nc, stride=SSM_CHUNK), :] = yg[rows, :LANES]
            y_ref[b, pl.ds(2 * t + 1, nc, stride=SSM_CHUNK), :] = yg[rows, LANES:]


def _ssm(u, tab_t, tab_in, tab_out, a_chunk, bb):
    b, l, _ = u.shape
    nc = l // SSM_CHUNK
    wspec = lambda a: pl.BlockSpec((1,) + a.shape[1:], lambda q, bi: (q,) + (0,) * (a.ndim - 1))
    return pl.pallas_call(
        functools.partial(_ssm_body, bb=bb, nc=nc),
        grid=(SSM_QBLOCKS, b // bb),
        in_specs=[pl.BlockSpec((bb, l, LANES), lambda q, bi: (bi, 0, q)),
                  wspec(tab_t), wspec(tab_in), wspec(tab_out), wspec(a_chunk)],
        out_specs=pl.BlockSpec((bb, l, LANES), lambda q, bi: (bi, 0, q)),
        out_shape=jax.ShapeDtypeStruct((b, l, SSM_WIDTH), F32),
        scratch_shapes=[pltpu.VMEM((N_PAIR, 2 * LANES, 2 * LANES), BF16),
                        pltpu.VMEM((N_PAIR, 2 * LANES, STATE_COLS), BF16),
                        pltpu.VMEM((N_PAIR, STATE_COLS, 2 * LANES), BF16),
                        pltpu.VMEM((bb * nc, STATE_COLS), F32),
                        pltpu.VMEM((bb * nc, STATE_COLS), F32),
                        pltpu.VMEM((bb * nc, STATE_COLS), F32)],
        compiler_params=pltpu.CompilerParams(dimension_semantics=("arbitrary", "arbitrary"),
                                             vmem_limit_bytes=VMEM_LIMIT),
        name="ssm",
    )(u, tab_t, tab_in, tab_out, a_chunk)


def _k4_body(x_ref, o_ref, ys_ref, g_ref, wap_ref, wglu_ref, wout_ref, gffn_ref, wrh_ref, wrl_ref,
             xn_ref, xsl_ref, route_ref, routet_ref, cnt_ref, *, tm):
    ts = tm // K4_SLABS
    slabs = [pl.ds(s * ts, ts) for s in range(K4_SLABS)]

    logits_all = []
    h2b_all = []
    for s, rows in enumerate(slabs):
        y_attn = _dot(o_ref[rows, :], wap_ref[...])
        vg = _dot(ys_ref[rows, :].astype(BF16), wglu_ref[...])
        y_ssm = vg[:, :D_MODEL] * jax.nn.sigmoid(vg[:, D_MODEL:])
        g = g_ref[rows, :].astype(F32)
        merged = g[:, :D_MODEL] * y_attn + g[:, D_MODEL:] * y_ssm
        xn = x_ref[rows, :] + _dot(merged.astype(BF16), wout_ref[...])
        xn_ref[rows, :] = xn
        h2 = _rms(xn, gffn_ref[...])
        hh = h2.astype(BF16)
        hl = (h2 - hh.astype(F32)).astype(BF16)
        h2b_all.append(hh)
        logits_all.append(_dot(hh, wrh_ref[...]) + _dot(hh, wrl_ref[...]) + _dot(hl, wrh_ref[...]))

    lane = lax.broadcasted_iota(jnp.int32, (ts, LANES), 1).astype(F32)
    ninf = jnp.float32(-jnp.inf)
    big = jnp.float32(4 * LANES)
    tri = (lax.broadcasted_iota(jnp.int32, (ts, ts), 1)
           < lax.broadcasted_iota(jnp.int32, (ts, ts), 0)).astype(BF16)
    seen = jnp.zeros((1, LANES), F32)
    picks = []
    for s, rows in enumerate(slabs):
        logits = logits_all[s]

        def top(mask):
            val = jnp.max(jnp.where(mask, logits, ninf), axis=-1, keepdims=True)
            idx = jnp.min(jnp.where(mask & (logits == val), lane, big), axis=-1, keepdims=True)
            return val, idx

        gmask = lane < N_GROUPS
        gmax, gidx = top(gmask)
        g_w = 1.0 / jnp.sum(jnp.where(gmask, jnp.exp(logits - gmax), 0.0), axis=-1, keepdims=True)
        lo = N_GROUPS + PER_GROUP * gidx
        emask = (lane >= lo) & (lane < lo + PER_GROUP)
        l1, i1 = top(emask)
        l2, i2 = top(emask & (lane != i1))
        e21 = jnp.exp(l2 - l1)
        w1 = g_w / (1.0 + e21)
        w2 = g_w * e21 / (1.0 + e21)

        hot1 = lane == i1
        hot2 = lane == i2
        onehot = (hot1 | hot2).astype(BF16)
        picks.append((w1, w2, i1, i2, hot1, hot2, _dot(tri, onehot) + seen))
        seen = seen + jnp.sum(onehot.astype(F32), axis=0, keepdims=True)
    cnt_ref[0] = seen

    padded = jnp.floor((seen + (CHUNK_ROWS - 1)) * (1.0 / CHUNK_ROWS)) * CHUNK_ROWS
    earlier = (lax.broadcasted_iota(jnp.int32, (LANES, LANES), 0)
               < lax.broadcasted_iota(jnp.int32, (LANES, LANES), 1)).astype(BF16)
    start = _dot(jnp.broadcast_to(padded, (SUBLANES, LANES)).astype(BF16), earlier)[0:1, :]

    local_rows = []
    for s, rows in enumerate(slabs):
        w1, w2, i1, i2, hot1, hot2, before = picks[s]
        r1 = jnp.sum(jnp.where(hot1, before + start, 0.0), axis=-1, keepdims=True)
        r2 = jnp.sum(jnp.where(hot2, before + start, 0.0), axis=-1, keepdims=True)
        cols = (w1, w2, i1 - N_GROUPS, i2 - N_GROUPS, r1, r2)
        out = jnp.zeros((ts, LANES), F32)
        for n, col in enumerate(cols):
            out = jnp.where(lane == n, col, out)
        route_ref[rows, :] = out
        out_t = out.T[:SUBLANES, :]
        routet_ref[:, rows] = out_t
        local_rows.append(out_t[4:6, :])

    lr = jnp.concatenate(local_rows, axis=1)
    n_local = xsl_ref.shape[0]
    row_id = lax.broadcasted_iota(jnp.int32, (n_local, tm), 0).astype(F32)
    select = ((row_id == lr[0:1, :]) | (row_id == lr[1:2, :])).astype(BF16)
    xsl_ref[...] = _dot(select, jnp.concatenate(h2b_all, axis=0))


def _k4(x, o, ys, gates, wap, wglu, wout, gffn, wrh, wrl, tm):
    t = x.shape[0]
    row = lambda w: pl.BlockSpec((tm, w), lambda i: (i, 0))
    full = lambda a: pl.BlockSpec(a.shape, lambda i: (0,) * a.ndim)
    return pl.pallas_call(
        functools.partial(_k4_body, tm=tm),
        grid=(t // tm,),
        in_specs=[row(D_MODEL), row(HEADS * V_DIM), row(SSM_WIDTH), row(2 * D_MODEL),
                  full(wap), full(wglu), full(wout), full(gffn), full(wrh), full(wrl)],
        out_specs=[row(D_MODEL), pl.BlockSpec((_local_rows(tm), D_MODEL), lambda i: (i, 0)), row(LANES),
                   pl.BlockSpec((SUBLANES, tm), lambda i: (0, i)),
                   pl.BlockSpec((1, 1, LANES), lambda i: (i, 0, 0))],
        out_shape=[jax.ShapeDtypeStruct((t, D_MODEL), F32),
                   jax.ShapeDtypeStruct((t // tm * _local_rows(tm), D_MODEL), F32),
                   jax.ShapeDtypeStruct((t, LANES), F32), jax.ShapeDtypeStruct((SUBLANES, t), F32),
                   jax.ShapeDtypeStruct((t // tm, 1, LANES), F32)],
        compiler_params=pltpu.CompilerParams(dimension_semantics=("arbitrary",),
                                             vmem_limit_bytes=VMEM_LIMIT),
        name="k4_merge_router",
    )(x, o, ys, gates, wap, wglu, wout, gffn, wrh, wrl)


def _row_copy(src, dst, sem):
    return pltpu.make_async_copy(src, dst, sem)


def _token_tile(ref, r):
    return ref.at[pl.ds(pl.multiple_of(r * ROW_BLOCKS, ROW_BLOCKS), ROW_BLOCKS)]


def _expert_body(te_ref, tv_ref, cs_ref, xsl_ref, wg_ref, wu_ref, wd_ref, ys_ref,
                 wgb_ref, wub_ref, wdb_ref, xbuf_ref, sem, *, n_steps):
    i = pl.program_id(0)
    slot = lax.rem(i, 2)

    def fetch(tile, into):
        def start(c2, _):
            for kk in range(2):
                c = 2 * c2 + kk
                chunk = cs_ref[tile * CHUNKS_PER_TILE + c]
                src = xsl_ref.at[pl.ds(pl.multiple_of(chunk * CHUNK_ROWS, CHUNK_ROWS), CHUNK_ROWS)]
                dst = xbuf_ref.at[into, pl.ds(pl.multiple_of(c * CHUNK_ROWS, CHUNK_ROWS), CHUNK_ROWS)]
                _row_copy(src, dst, sem.at[into]).start(priority=kk)
            return 0
        lax.fori_loop(0, CHUNKS_PER_TILE // 2, start, 0, unroll=ROW_UNROLL // 2)

    @pl.when(i == 0)
    def _():
        fetch(0, 0)

    @pl.when((i + 1 < n_steps) & (tv_ref[jnp.minimum(i + 1, n_steps - 1)] == 1))
    def _():
        fetch(i + 1, 1 - slot)

    prev = te_ref[jnp.maximum(i - 1, 0)]

    @pl.when((i == 0) | (te_ref[i] != prev))
    def _():
        wgb_ref[...] = wg_ref[0, 0].astype(BF16)
        wub_ref[...] = wu_ref[0, 0].astype(BF16)
        wdb_ref[...] = wd_ref[0, 0].astype(BF16)

    @pl.when(tv_ref[i] == 1)
    def _():
        _row_copy(xsl_ref.at[pl.ds(0, ROW_TILE)], xbuf_ref.at[slot], sem.at[slot]).wait()
        xb = xbuf_ref[slot].astype(BF16)
        hidden = jax.nn.silu(_dot(xb, wgb_ref[...])) * _dot(xb, wub_ref[...])
        _to_token_tiles(ys_ref, _dot(hidden.astype(BF16), wdb_ref[...]), ROW_TILE)

    @pl.when(tv_ref[i] == 0)
    def _():
        ys_ref[...] = jnp.zeros_like(ys_ref)


def _experts(tile_expert, tile_valid, chunk_src, xsl, wg, wu, wd, layer):
    n_steps = tile_expert.shape[0]
    tile_rows = ROW_TILE * ROW_BLOCKS
    grid_spec = pltpu.PrefetchScalarGridSpec(
        num_scalar_prefetch=3,
        grid=(n_steps,),
        in_specs=[pl.BlockSpec(memory_space=pl.ANY),
                  pl.BlockSpec((1, 1, D_MODEL, D_EXPERT), lambda i, te, tv, cs: (layer, te[i], 0, 0)),
                  pl.BlockSpec((1, 1, D_MODEL, D_EXPERT), lambda i, te, tv, cs: (layer, te[i], 0, 0)),
                  pl.BlockSpec((1, 1, D_EXPERT, D_MODEL), lambda i, te, tv, cs: (layer, te[i], 0, 0))],
        out_specs=pl.BlockSpec((tile_rows, LANES), lambda i, te, tv, cs: (i, 0)),
        scratch_shapes=[pltpu.VMEM((D_MODEL, D_EXPERT), BF16), pltpu.VMEM((D_MODEL, D_EXPERT), BF16),
                        pltpu.VMEM((D_EXPERT, D_MODEL), BF16), pltpu.VMEM((2, ROW_TILE, D_MODEL), F32),
                        pltpu.SemaphoreType.DMA((2,))],
    )
    return pl.pallas_call(
        functools.partial(_expert_body, n_steps=n_steps),
        grid_spec=grid_spec,
        out_shape=jax.ShapeDtypeStruct((n_steps * tile_rows, LANES), F32),
        compiler_params=pltpu.CompilerParams(dimension_semantics=("arbitrary",),
                                             vmem_limit_bytes=VMEM_LIMIT),
        name="moe_experts",
    )(tile_expert, tile_valid, chunk_src, xsl, wg, wu, wd)


def _combine_body(pos_ref, nxt_ref, x_ref, route_ref, ys_ref, out_ref, buf_ref, sem, *, tc, n_steps):
    i = pl.program_id(0)
    slot = lax.rem(i, 2)

    def fetch(rows_of, into):
        def start(r, _):
            for kk in range(2):
                _row_copy(_token_tile(ys_ref, rows_of[0, kk, r]), _token_tile(buf_ref.at[into, kk], r),
                          sem.at[into]).start(priority=kk)
            return 0
        lax.fori_loop(0, tc, start, 0, unroll=ROW_UNROLL)

    @pl.when(i == 0)
    def _():
        fetch(pos_ref, 0)

    @pl.when(i + 1 < n_steps)
    def _():
        fetch(nxt_ref, 1 - slot)

    for kk in range(2):
        _row_copy(ys_ref.at[pl.ds(0, tc * ROW_BLOCKS)], buf_ref.at[slot, kk], sem.at[slot]).wait()
    route = route_ref[...]
    out_ref[...] = (x_ref[...] + route[:, 0:1] * _from_token_tiles(buf_ref.at[slot, 0], tc)
                    + route[:, 1:2] * _from_token_tiles(buf_ref.at[slot, 1], tc))


def _combine(pos, x, route, ys, tc):
    t = x.shape[0]
    n_steps = t // tc
    return pl.pallas_call(
        functools.partial(_combine_body, tc=tc, n_steps=n_steps),
        grid=(n_steps,),
        in_specs=[pl.BlockSpec((1, 2, tc), lambda i: (i, 0, 0), memory_space=pltpu.SMEM),
                  pl.BlockSpec((1, 2, tc), lambda i: (jnp.minimum(i + 1, n_steps - 1), 0, 0),
                               memory_space=pltpu.SMEM),
                  pl.BlockSpec((tc, D_MODEL), lambda i: (i, 0)),
                  pl.BlockSpec((tc, LANES), lambda i: (i, 0)),
                  pl.BlockSpec(memory_space=pl.ANY)],
        out_specs=pl.BlockSpec((tc, D_MODEL), lambda i: (i, 0)),
        out_shape=jax.ShapeDtypeStruct((t, D_MODEL), F32),
        scratch_shapes=[pltpu.VMEM((2, 2, tc * ROW_BLOCKS, LANES), F32), pltpu.SemaphoreType.DMA((2,))],
        compiler_params=pltpu.CompilerParams(dimension_semantics=("arbitrary",)),
        name="moe_combine",
    )(pos, pos, x, route, ys)


def _expert_tiles(t, tm):
    return (2 * t + (t // tm) * N_EXPERTS * (CHUNK_ROWS - 1)) // ROW_TILE + N_EXPERTS


def _routing_plan(route_t, counts, tm):
    i32 = jnp.int32
    t = route_t.shape[1]
    n_tok_tiles = t // tm
    local_chunks = _local_rows(tm) // CHUNK_ROWS
    n_tiles = _expert_tiles(t, tm)
    cnt = counts[:, 0, N_GROUPS:N_GROUPS + N_EXPERTS].astype(i32)
    chunks = (cnt + CHUNK_ROWS - 1) // CHUNK_ROWS
    first_local = jnp.cumsum(chunks, axis=1) - chunks
    upto = jnp.cumsum(chunks, axis=0)
    before = upto - chunks
    n_chunks = upto[-1]
    n_exp_tiles = (n_chunks + CHUNKS_PER_TILE - 1) // CHUNKS_PER_TILE
    tile_end = jnp.cumsum(n_exp_tiles)
    tile_start = tile_end - n_exp_tiles
    used = tile_end[-1]
    tiles = jnp.arange(n_tiles, dtype=i32)
    valid = tiles < used
    last = jnp.maximum(used - 1, 0)
    expert_of = jnp.sum((tile_end[None, :] <= tiles[:, None]).astype(i32), axis=1)
    tile_expert = jnp.where(valid, expert_of, jnp.sum(jnp.where(tiles == last, expert_of, 0)))

    sel = tile_expert[:, None] == jnp.arange(N_EXPERTS, dtype=i32)[None, :]
    per_tile = lambda table: jnp.sum(jnp.where(sel[:, :, None], table.T[None], 0), axis=1)
    per_tile1 = lambda vec: jnp.sum(jnp.where(sel, vec[None, :], 0), axis=1)
    upto_g, before_g, first_g = per_tile(upto), per_tile(before), per_tile(first_local)
    k = ((tiles - per_tile1(tile_start))[:, None] * CHUNKS_PER_TILE
         + jnp.arange(CHUNKS_PER_TILE, dtype=i32)[None, :])
    k_valid = valid[:, None] & (k < per_tile1(n_chunks)[:, None])
    src_tile = jnp.minimum(jnp.sum((upto_g[:, None, :] <= k[:, :, None]).astype(i32), axis=2), n_tok_tiles - 1)
    at = src_tile[:, :, None] == jnp.arange(n_tok_tiles, dtype=i32)[None, None, :]
    before_at = jnp.sum(jnp.where(at, before_g[:, None, :], 0), axis=2)
    first_at = jnp.sum(jnp.where(at, first_g[:, None, :], 0), axis=2)
    chunk = src_tile * local_chunks + first_at + (k - before_at)
    chunk_src = jnp.where(k_valid, chunk, local_chunks - 1).reshape(-1).astype(i32)

    base = tile_start[None, :] * ROW_TILE + CHUNK_ROWS * (before - first_local)
    base_t = jnp.repeat(base.T, tm, axis=1)
    experts = jnp.arange(N_EXPERTS, dtype=i32)[:, None]

    def position(e_row, local_row):
        e = e_row.astype(i32)[None, :]
        return jnp.sum(jnp.where(e == experts, base_t, 0), axis=0) + local_row.astype(i32)

    pos = jnp.stack([position(route_t[2], route_t[4]).reshape(n_tok_tiles, tm),
                     position(route_t[3], route_t[5]).reshape(n_tok_tiles, tm)], axis=1)
    return pos, tile_expert, valid.astype(i32), chunk_src


def _rope_tables(positions):
    half = ROPE // 2
    inv_freq = ROPE_THETA ** (-jnp.arange(half, dtype=F32) / half)
    ang = inv_freq[:, None] * positions.astype(F32).reshape(1, -1)
    spread = lambda a: jnp.tile(a.T, (1, LANES // half))
    return spread(jnp.cos(ang)), spread(jnp.sin(ang))


def _head_slots(w, width):
    k = w.shape[0]
    w = w.reshape(k, HEADS, width)
    return jnp.pad(w, ((0, 0), (0, 0), (0, LANES - width))).reshape(k, HEADS * LANES)


def _layer_weights(wi, w_uq, w_ukv, q_head_g, k_head_g, w_router_group, w_router_expert,
                   w_attn_proj, w_glu, w_out):
    kr = jnp.pad(wi[:, C_KR:C_KR + ROPE], ((0, 0), (NOPE, LANES - QK_DIM)))
    win = jnp.concatenate([wi[:, :C_KR], kr, wi[:, C_KR + ROPE:]], axis=1).astype(BF16)
    wuq = _head_slots(w_uq, QK_DIM).astype(BF16)
    kv = w_ukv.reshape(KV_LORA, HEADS, NOPE + V_DIM)
    wuk = _head_slots(kv[:, :, :NOPE].reshape(KV_LORA, HEADS * NOPE), NOPE)
    wuv = _head_slots(kv[:, :, NOPE:].reshape(KV_LORA, HEADS * V_DIM), V_DIM)
    wukv = jnp.concatenate([wuk, wuv], axis=1).astype(BF16)
    pad_g = lambda g: jnp.tile(jnp.pad(g.astype(F32), (0, LANES - QK_DIM)), HEADS)[None, :]
    gq = pad_g(q_head_g) * (QK_DIM ** -0.5 * math.log2(math.e))
    gk = pad_g(k_head_g)
    wr = jnp.concatenate([w_router_group, w_router_expert], axis=1).astype(F32)
    wr = jnp.pad(wr, ((0, 0), (0, LANES - wr.shape[1])))
    wrh = wr.astype(BF16)
    wrl = (wr - wrh.astype(F32)).astype(BF16)
    return (win, wuq, wukv, gq, gk, wrh, wrl,
            w_attn_proj.astype(BF16), w_glu.astype(BF16), w_out.astype(BF16))


def kernel(x, positions, norm_mix_g, w_in, q_lora_g, w_uq, kv_lora_g, w_ukv, q_head_g, k_head_g, w_attn_proj, lam_re, lam_im, log_dt, b_re, b_im, c_re, c_im, d_skip, w_glu, w_out, norm_ffn_g, w_router_group, w_router_expert, w_exp_gate, w_exp_up, w_exp_down):
    bsz, seq, _ = x.shape
    t = bsz * seq
    depth = w_in.shape[0]
    tm = 512 if t % 512 == 0 else 256
    tq = 256
    nsub = 2 if seq % (2 * tq) == 0 else 1
    bb = 4 if bsz % 4 == 0 else (2 if bsz % 2 == 0 else 1)
    row1 = lambda g: g.astype(F32)[None, :]

    cos_t, sin_t = _rope_tables(positions)
    dense = jax.vmap(_layer_weights)(w_in, w_uq, w_ukv, q_head_g, k_head_g, w_router_group,
                                     w_router_expert, w_attn_proj, w_glu, w_out)
    ssm_tables = jax.vmap(_ssm_tables)(lam_re, lam_im, log_dt, b_re, b_im, c_re, c_im, d_skip)
    xf = x.reshape(t, D_MODEL).astype(F32)
    for l in range(depth):
        win, wuq, wukv, gq, gk, wrh, wrl, wap, wglu, wout = [w[l] for w in dense]
        q, k, v, u, gates = _k1(xf, row1(norm_mix_g[l]), win, row1(q_lora_g[l]), wuq,
                                row1(kv_lora_g[l]), wukv, gq, gk, cos_t, sin_t, tm)
        o = _attention(q.reshape(bsz, seq, -1), k.reshape(bsz, seq, -1), v.reshape(bsz, seq, -1), tq, nsub)
        ys = _ssm(u.reshape(bsz, seq, SSM_WIDTH), *[tab[l] for tab in ssm_tables], bb)
        xn, xsl, route, route_t, counts = _k4(xf, o.reshape(t, -1), ys.reshape(t, SSM_WIDTH), gates,
                                              wap, wglu, wout, row1(norm_ffn_g[l]), wrh, wrl, tm)
        pos, tile_expert, tile_valid, chunk_src = _routing_plan(route_t, counts, tm)
        ye = _experts(tile_expert, tile_valid, chunk_src, xsl, w_exp_gate, w_exp_up, w_exp_down, l)
        xf = _combine(pos, xn, route, ye, tm)
    return xf.reshape(bsz, seq, D_MODEL).astype(x.dtype)
```

```python
import functools
import math

import jax
import jax.numpy as jnp
from jax import lax
from jax.experimental import pallas as pl
from jax.experimental.pallas import tpu as pltpu

F32 = jnp.float32
BF16 = jnp.bfloat16

D_MODEL = 1024
CHUNK = 64
HEADS = 8
NOPE = 64
ROPE = 32
QK_DIM = NOPE + ROPE
V_DIM = 64
Q_LORA = 384
KV_LORA = 256
ROPE_THETA = 10000.0
SSM_WIDTH = 512
SSM_GROUP = 16
SSM_GROUPS = 32
SSM_STATE = 64
N_GROUPS = 4
PER_GROUP = 8
N_EXPERTS = 32
D_EXPERT = 256
EPS = 1e-6

LANES = 128
SUBLANES = 8
SSM_CHUNK = 16
N_PAIR = SSM_CHUNK // 2
SSM_QBLOCKS = SSM_WIDTH // LANES
GROUPS_PER_QBLOCK = LANES // SSM_GROUP
STATE_COLS = GROUPS_PER_QBLOCK * 2 * SSM_STATE
ROW_TILE = 512
ROW_UNROLL = 8
K1_SLABS = 1
K4_SLABS = 2
ROW_BLOCKS = D_MODEL // LANES
CHUNK_ROWS = SUBLANES
CHUNKS_PER_TILE = ROW_TILE // CHUNK_ROWS
VMEM_LIMIT = 56 * 1024 * 1024


def _local_rows(tm):
    return 2 * tm + N_EXPERTS * CHUNK_ROWS

C_Q = 0
C_KV = C_Q + Q_LORA
C_KR = C_KV + KV_LORA
C_U = C_KR + LANES
C_GATE = C_U + SSM_WIDTH
IN_COLS_PAD = C_GATE + 2 * D_MODEL


def _rms(x, g):
    return x * lax.rsqrt(jnp.mean(x * x, axis=-1, keepdims=True) + EPS) * g


def _dot(a, b):
    return jnp.dot(a, b, preferred_element_type=F32)


def _to_token_tiles(ref, x, n, base=0):
    for k in range(ROW_BLOCKS):
        ref[pl.ds(base * ROW_BLOCKS + k, n, stride=ROW_BLOCKS), :] = x[:, k * LANES:(k + 1) * LANES]


def _from_token_tiles(ref, n):
    return jnp.concatenate([ref[pl.ds(k, n, stride=ROW_BLOCKS), :] for k in range(ROW_BLOCKS)], axis=1)


def _k1_body(x_ref, gmix_ref, win_ref, gql_ref, wuq_ref, gkvl_ref, wukv_ref, gq_ref, gk_ref,
             cos_ref, sin_ref, q_ref, k_ref, v_ref, u_ref, gates_ref):
    vlane = lax.broadcasted_iota(jnp.int32, (1, HEADS * LANES), 1)
    ones_col = ((vlane % LANES) == V_DIM).astype(F32)
    ts = x_ref.shape[0] // K1_SLABS
    for s in range(K1_SLABS):
        rows = pl.ds(s * ts, ts)
        hb = _rms(x_ref[rows, :], gmix_ref[...]).astype(BF16)

        def proj(lo, hi):
            return _dot(hb, win_ref[:, lo:hi])

        lane = lax.broadcasted_iota(jnp.int32, (ts, LANES), 1)
        sin = sin_ref[rows, :]
        cos_t = jnp.where(lane < NOPE, 1.0, jnp.where(lane < QK_DIM, cos_ref[rows, :], 0.0))
        sin_m = jnp.where((lane >= NOPE) & (lane < NOPE + ROPE // 2), -sin, 0.0)
        sin_p = jnp.where((lane >= NOPE + ROPE // 2) & (lane < QK_DIM), sin, 0.0)

        def head_norm_rope(r, g):
            ms = jnp.sum(r * r, axis=-1, keepdims=True) * (1.0 / QK_DIM)
            rn = r * lax.rsqrt(ms + EPS) * g
            return (rn * cos_t + pltpu.roll(rn, LANES - ROPE // 2, 1) * sin_m
                    + pltpu.roll(rn, ROPE // 2, 1) * sin_p)

        qn = _rms(proj(C_Q, C_KV), gql_ref[...]).astype(BF16)
        qf = _dot(qn, wuq_ref[...])
        for h in range(HEADS):
            sl = slice(h * LANES, (h + 1) * LANES)
            q_ref[rows, sl] = head_norm_rope(qf[:, sl], gq_ref[:, sl]).astype(BF16)

        kvn = _rms(proj(C_KV, C_KR), gkvl_ref[...]).astype(BF16)
        kf = _dot(kvn, wukv_ref[...])
        kr = proj(C_KR, C_U)
        for h in range(HEADS):
            sl = slice(h * LANES, (h + 1) * LANES)
            k_ref[rows, sl] = head_norm_rope(kf[:, sl] + kr, gk_ref[:, sl]).astype(BF16)
        v_ref[rows, :] = (kf[:, HEADS * LANES:] + ones_col).astype(BF16)

        u_ref[rows, :] = proj(C_U, C_GATE)
        gates_ref[rows, :] = jax.nn.sigmoid(proj(C_GATE, IN_COLS_PAD)).astype(BF16)


def _k1(x, gmix, win, gql, wuq, gkvl, wukv, gq, gk, cos_t, sin_t, tm):
    t = x.shape[0]
    row = lambda w: pl.BlockSpec((tm, w), lambda i: (i, 0))
    full = lambda a: pl.BlockSpec(a.shape, lambda i: (0,) * a.ndim)
    return pl.pallas_call(
        _k1_body,
        grid=(t // tm,),
        in_specs=[row(D_MODEL), full(gmix), full(win), full(gql), full(wuq), full(gkvl), full(wukv),
                  full(gq), full(gk), row(LANES), row(LANES)],
        out_specs=[row(HEADS * LANES), row(HEADS * LANES), row(HEADS * LANES), row(SSM_WIDTH),
                   row(2 * D_MODEL)],
        out_shape=[jax.ShapeDtypeStruct((t, HEADS * LANES), BF16),
                   jax.ShapeDtypeStruct((t, HEADS * LANES), BF16),
                   jax.ShapeDtypeStruct((t, HEADS * LANES), BF16),
                   jax.ShapeDtypeStruct((t, SSM_WIDTH), F32),
                   jax.ShapeDtypeStruct((t, 2 * D_MODEL), BF16)],
        compiler_params=pltpu.CompilerParams(dimension_semantics=("arbitrary",),
                                             vmem_limit_bytes=VMEM_LIMIT),
        name="k1_inproj",
    )(x, gmix, win, gql, wuq, gkvl, wukv, gq, gk, cos_t, sin_t)


def _attn_body(q_ref, k_ref, v_ref, o_ref, *, tq, nsub, nq):
    i = pl.program_id(1)
    row_chunk = lax.broadcasted_iota(jnp.int32, (tq, tq), 0) // CHUNK
    col_chunk = lax.broadcasted_iota(jnp.int32, (tq, tq), 1) // CHUNK
    diag_mask = col_chunk <= row_chunk
    neg = jnp.float32(-1e30)

    def scores(rows, hh, n_full):
        sl = slice(hh * LANES, (hh + 1) * LANES)
        s = lax.dot_general(q_ref[0, rows, sl], k_ref[0, :n_full + tq, sl], (((1,), (1,)), ((), ())),
                            preferred_element_type=F32)
        s_diag = jnp.where(diag_mask, s[:, n_full:], neg)
        return jnp.concatenate([s[:, :n_full], s_diag], axis=1) if n_full else s_diag

    def probs(s):
        return jnp.exp2(s - jnp.max(s, axis=-1, keepdims=True)).astype(BF16)

    def values(p, hh, n_full):
        of = _dot(p, v_ref[0, :n_full + tq, hh * LANES:(hh + 1) * LANES])
        return of * (1.0 / of[:, V_DIM:V_DIM + 1])

    for ii in range(nq):
        @pl.when(i == ii)
        def _(ii=ii):
            lane = lax.broadcasted_iota(jnp.int32, (tq, LANES), 1)
            work = [(pl.ds(sub * tq, tq), (ii * nsub + sub) * tq) for sub in range(nsub)]
            ss = [[scores(rows, hh, n_full) for hh in range(HEADS)] for rows, n_full in work]
            ps = [[probs(s) for s in per_block] for per_block in ss]
            for (rows, n_full), per_block in zip(work, ps):
                os_ = [values(p, hh, n_full) for hh, p in enumerate(per_block)]
                for hp in range(HEADS // 2):
                    o_ref[0, rows, hp * LANES:(hp + 1) * LANES] = jnp.where(
                        lane < V_DIM, os_[2 * hp], pltpu.roll(os_[2 * hp + 1], V_DIM, 1)).astype(BF16)


def _attention(q, k, v, tq, nsub):
    b, l, _ = q.shape
    nq = l // (tq * nsub)
    return pl.pallas_call(
        functools.partial(_attn_body, tq=tq, nsub=nsub, nq=nq),
        grid=(b, nq),
        in_specs=[pl.BlockSpec((1, tq * nsub, HEADS * LANES), lambda bi, i: (bi, i, 0)),
                  pl.BlockSpec((1, l, HEADS * LANES), lambda bi, i: (bi, 0, 0)),
                  pl.BlockSpec((1, l, HEADS * LANES), lambda bi, i: (bi, 0, 0))],
        out_specs=pl.BlockSpec((1, tq * nsub, HEADS * V_DIM), lambda bi, i: (bi, i, 0)),
        out_shape=jax.ShapeDtypeStruct((b, l, HEADS * V_DIM), BF16),
        compiler_params=pltpu.CompilerParams(
            dimension_semantics=("arbitrary", "arbitrary"), vmem_limit_bytes=VMEM_LIMIT),
        name="attention",
    )(q, k, v)


def _cmul(a, b):
    return a[0] * b[0] - a[1] * b[1], a[0] * b[1] + a[1] * b[0]


def _ssm_tables(lam_re, lam_im, log_dt, b_re, b_im, c_re, c_im, d_skip):
    lc = SSM_CHUNK
    nq, gq = SSM_QBLOCKS, GROUPS_PER_QBLOCK
    lr, li = lam_re.astype(F32), lam_im.astype(F32)
    dt = jnp.exp(log_dt.astype(F32))[:, None]
    steps = jnp.arange(lc + 1, dtype=F32)[:, None, None]
    mag = jnp.exp(lr * dt * steps)
    pw = (mag * jnp.cos(li * dt * steps), mag * jnp.sin(li * dt * steps))
    num = (pw[0][1] - 1.0, pw[1][1])
    den = lr * lr + li * li
    ratio = ((num[0] * lr + num[1] * li) / den, (num[1] * lr - num[0] * li) / den)
    b_bar = _cmul((ratio[0][..., None], ratio[1][..., None]), (b_re.astype(F32), b_im.astype(F32)))
    c = (c_re.astype(F32), c_im.astype(F32))

    m = _cmul((pw[0][:lc, :, :, None], pw[1][:lc, :, :, None]), (b_bar[0][None], b_bar[1][None]))
    kern = jnp.einsum('gip,dgpj->dgij', c[0], m[0]) - jnp.einsum('gip,dgpj->dgij', c[1], m[1])
    kern = kern.at[0].add(jnp.eye(SSM_GROUP, dtype=F32)[None] * d_skip.astype(F32)[:, :, None])
    kpad = jnp.concatenate([jnp.zeros_like(kern[:1]), kern], axis=0)
    big_d = jnp.arange(N_PAIR)
    rows = []
    for s2 in range(2):
        cols = []
        for t2 in range(2):
            kd = kpad[2 * big_d + t2 - s2 + 1].reshape(N_PAIR, nq, gq, SSM_GROUP, SSM_GROUP)
            cols.append(jnp.transpose(kd, (1, 0, 4, 2, 3)))
        rows.append(jnp.stack(cols, axis=3))
    tab_t = jnp.stack(rows, axis=2).reshape(nq, N_PAIR, 2, SSM_GROUP, 2 * LANES)

    bt = (jnp.swapaxes(b_bar[0], 1, 2), jnp.swapaxes(b_bar[1], 1, 2))
    vin = _cmul((pw[0][:lc][::-1][:, :, None, :], pw[1][:lc][::-1][:, :, None, :]), (bt[0][None], bt[1][None]))
    vin = jnp.concatenate(vin, axis=-1).reshape(N_PAIR, 2, nq, gq, SSM_GROUP, 2 * SSM_STATE)
    tab_in = jnp.transpose(vin, (2, 0, 1, 4, 3, 5)).reshape(nq, N_PAIR, 2, SSM_GROUP, STATE_COLS)

    cw = _cmul((c[0][None], c[1][None]), (pw[0][1:lc + 1][:, :, None, :], pw[1][1:lc + 1][:, :, None, :]))
    cw = jnp.concatenate([cw[0], -cw[1]], axis=-1).reshape(N_PAIR, 2, nq, gq, SSM_GROUP, 2 * SSM_STATE)
    tab_out = jnp.transpose(cw, (2, 0, 1, 3, 4, 5)).reshape(nq, N_PAIR, 2 * LANES, 2 * SSM_STATE)

    ar = pw[0][lc].reshape(nq, gq, 1, SSM_STATE)
    ai = pw[1][lc].reshape(nq, gq, 1, SSM_STATE)
    a_same = jnp.concatenate([ar, ar], axis=2).reshape(nq, 1, STATE_COLS)
    a_swap = jnp.concatenate([-ai, ai], axis=2).reshape(nq, 1, STATE_COLS)
    return tab_t, tab_in, tab_out, jnp.concatenate([a_same, a_swap], axis=1)


def _ssm_body(u_ref, tt_ref, tin_ref, tout_ref, a_ref, y_ref, wt_ref, win_ref, wout_ref, st_ref, sw_ref,
              hp_ref, *, bb, nc):
    gq = GROUPS_PER_QBLOCK

    @pl.when(pl.program_id(1) == 0)
    def _():
        grp_t = (lax.broadcasted_iota(jnp.int32, (SSM_GROUP, 2 * LANES), 1) % LANES) // SSM_GROUP
        grp_in = lax.broadcasted_iota(jnp.int32, (SSM_GROUP, STATE_COLS), 1) // LANES
        grp_out = (lax.broadcasted_iota(jnp.int32, (LANES, 2 * LANES), 1) % LANES) // SSM_GROUP
        for d in range(N_PAIR):
            for s2 in range(2):
                piece_t = tt_ref[0, d, s2]
                piece_in = tin_ref[0, d, s2]
                for gl in range(gq):
                    rows = pl.ds(s2 * LANES + gl * SSM_GROUP, SSM_GROUP)
                    wt_ref[d, rows, :] = jnp.where(grp_t == gl, piece_t, 0.0).astype(BF16)
                    win_ref[d, rows, :] = jnp.where(grp_in == gl, piece_in, 0.0).astype(BF16)
            piece_out = tout_ref[0, d].T
            for gl in range(gq):
                wout_ref[d, pl.ds(gl * LANES, LANES), :] = jnp.where(grp_out == gl, piece_out, 0.0).astype(BF16)

    xs = []
    for s in range(N_PAIR):
        per_batch = []
        for b in range(bb):
            x0 = u_ref[b, pl.ds(2 * s, nc, stride=SSM_CHUNK), :]
            x1 = u_ref[b, pl.ds(2 * s + 1, nc, stride=SSM_CHUNK), :]
            per_batch.append(jnp.concatenate([x0, x1], axis=1))
        xs.append(jnp.concatenate(per_batch, axis=0).astype(BF16))

    st = _dot(xs[0], win_ref[0])
    for s in range(1, N_PAIR):
        st = st + _dot(xs[s], win_ref[s])
    st_ref[...] = st
    sw_ref[...] = jnp.concatenate(
        [pltpu.roll(st[:, g * LANES:(g + 1) * LANES], SSM_STATE, 1) for g in range(gq)], axis=1)

    a_same = a_ref[0, 0:1, :]
    a_swap = a_ref[0, 1:2, :]

    def step(c, carry):
        new = []
        for b in range(bb):
            h, hs = carry[b]
            row = b * nc + c
            hp_ref[pl.ds(row, 1), :] = h
            new.append((a_same * h + a_swap * hs + st_ref[pl.ds(row, 1), :],
                        a_same * hs - a_swap * h + sw_ref[pl.ds(row, 1), :]))
        return tuple(new)

    zero = jnp.zeros((1, STATE_COLS), F32)
    lax.fori_loop(0, nc, step, tuple((zero, zero) for _ in range(bb)))

    hp = hp_ref[...].astype(BF16)
    for t in range(N_PAIR):
        acc = _dot(hp, wout_ref[t])
        for s in range(t + 1):
            acc = acc + _dot(xs[s], wt_ref[t - s])
        yg = jax.nn.gelu(acc)
        for b in range(bb):
            rows = slice(b * nc, (b + 1) * nc)
            y_ref[b, pl.ds(2 * t, nc, stride=SSM_CHUNK), :] = yg[rows, :LANES]
            y_ref[b, pl.ds(2 * t + 1, nc, stride=SSM_CHUNK), :] = yg[rows, LANES:]


def _ssm(u, tab_t, tab_in, tab_out, a_chunk, bb):
    b, l, _ = u.shape
    nc = l // SSM_CHUNK
    wspec = lambda a: pl.BlockSpec((1,) + a.shape[1:], lambda q, bi: (q,) + (0,) * (a.ndim - 1))
    return pl.pallas_call(
        functools.partial(_ssm_body, bb=bb, nc=nc),
        grid=(SSM_QBLOCKS, b // bb),
        in_specs=[pl.BlockSpec((bb, l, LANES), lambda q, bi: (bi, 0, q)),
                  wspec(tab_t), wspec(tab_in), wspec(tab_out), wspec(a_chunk)],
        out_specs=pl.BlockSpec((bb, l, LANES), lambda q, bi: (bi, 0, q)),
        out_shape=jax.ShapeDtypeStruct((b, l, SSM_WIDTH), F32),
        scratch_shapes=[pltpu.VMEM((N_PAIR, 2 * LANES, 2 * LANES), BF16),
                        pltpu.VMEM((N_PAIR, 2 * LANES, STATE_COLS), BF16),
                        pltpu.VMEM((N_PAIR, STATE_COLS, 2 * LANES), BF16),
                        pltpu.VMEM((bb * nc, STATE_COLS), F32),
                        pltpu.VMEM((bb * nc, STATE_COLS), F32),
                        pltpu.VMEM((bb * nc, STATE_COLS), F32)],
        compiler_params=pltpu.CompilerParams(dimension_semantics=("arbitrary", "arbitrary"),
                                             vmem_limit_bytes=VMEM_LIMIT),
        name="ssm",
    )(u, tab_t, tab_in, tab_out, a_chunk)


def _k4_body(x_ref, o_ref, ys_ref, g_ref, wap_ref, wglu_ref, wout_ref, gffn_ref, wrh_ref, wrl_ref,
             xn_ref, xsl_ref, route_ref, routet_ref, cnt_ref, *, tm):
    ts = tm // K4_SLABS
    slabs = [pl.ds(s * ts, ts) for s in range(K4_SLABS)]

    logits_all = []
    h2b_all = []
    for s, rows in enumerate(slabs):
        y_attn = _dot(o_ref[rows, :], wap_ref[...])
        vg = _dot(ys_ref[rows, :].astype(BF16), wglu_ref[...])
        y_ssm = vg[:, :D_MODEL] * jax.nn.sigmoid(vg[:, D_MODEL:])
        g = g_ref[rows, :].astype(F32)
        merged = g[:, :D_MODEL] * y_attn + g[:, D_MODEL:] * y_ssm
        xn = x_ref[rows, :] + _dot(merged.astype(BF16), wout_ref[...])
        xn_ref[rows, :] = xn
        h2 = _rms(xn, gffn_ref[...])
        hh = h2.astype(BF16)
        hl = (h2 - hh.astype(F32)).astype(BF16)
        h2b_all.append(hh)
        logits_all.append(_dot(hh, wrh_ref[...]) + _dot(hh, wrl_ref[...]) + _dot(hl, wrh_ref[...]))

    lane = lax.broadcasted_iota(jnp.int32, (ts, LANES), 1).astype(F32)
    ninf = jnp.float32(-jnp.inf)
    big = jnp.float32(4 * LANES)
    tri = (lax.broadcasted_iota(jnp.int32, (ts, ts), 1)
           < lax.broadcasted_iota(jnp.int32, (ts, ts), 0)).astype(BF16)
    seen = jnp.zeros((1, LANES), F32)
    picks = []
    for s, rows in enumerate(slabs):
        logits = logits_all[s]

        def top(mask):
            val = jnp.max(jnp.where(mask, logits, ninf), axis=-1, keepdims=True)
            idx = jnp.min(jnp.where(mask & (logits == val), lane, big), axis=-1, keepdims=True)
            return val, idx

        gmask = lane < N_GROUPS
        gmax, gidx = top(gmask)
        g_w = 1.0 / jnp.sum(jnp.where(gmask, jnp.exp(logits - gmax), 0.0), axis=-1, keepdims=True)
        lo = N_GROUPS + PER_GROUP * gidx
        emask = (lane >= lo) & (lane < lo + PER_GROUP)
        l1, i1 = top(emask)
        l2, i2 = top(emask & (lane != i1))
        e21 = jnp.exp(l2 - l1)
        w1 = g_w / (1.0 + e21)
        w2 = g_w * e21 / (1.0 + e21)

        hot1 = lane == i1
        hot2 = lane == i2
        onehot = (hot1 | hot2).astype(BF16)
        picks.append((w1, w2, i1, i2, hot1, hot2, _dot(tri, onehot) + seen))
        seen = seen + jnp.sum(onehot.astype(F32), axis=0, keepdims=True)
    cnt_ref[0] = seen

    padded = jnp.floor((seen + (CHUNK_ROWS - 1)) * (1.0 / CHUNK_ROWS)) * CHUNK_ROWS
    earlier = (lax.broadcasted_iota(jnp.int32, (LANES, LANES), 0)
               < lax.broadcasted_iota(jnp.int32, (LANES, LANES), 1)).astype(BF16)
    start = _dot(jnp.broadcast_to(padded, (SUBLANES, LANES)).astype(BF16), earlier)[0:1, :]

    local_rows = []
    for s, rows in enumerate(slabs):
        w1, w2, i1, i2, hot1, hot2, before = picks[s]
        r1 = jnp.sum(jnp.where(hot1, before + start, 0.0), axis=-1, keepdims=True)
        r2 = jnp.sum(jnp.where(hot2, before + start, 0.0), axis=-1, keepdims=True)
        cols = (w1, w2, i1 - N_GROUPS, i2 - N_GROUPS, r1, r2)
        out = jnp.zeros((ts, LANES), F32)
        for n, col in enumerate(cols):
            out = jnp.where(lane == n, col, out)
        route_ref[rows, :] = out
        out_t = out.T[:SUBLANES, :]
        routet_ref[:, rows] = out_t
        local_rows.append(out_t[4:6, :])

    lr = jnp.concatenate(local_rows, axis=1)
    n_local = xsl_ref.shape[0]
    row_id = lax.broadcasted_iota(jnp.int32, (n_local, tm), 0).astype(F32)
    select = ((row_id == lr[0:1, :]) | (row_id == lr[1:2, :])).astype(BF16)
    xsl_ref[...] = _dot(select, jnp.concatenate(h2b_all, axis=0))


def _k4(x, o, ys, gates, wap, wglu, wout, gffn, wrh, wrl, tm):
    t = x.shape[0]
    row = lambda w: pl.BlockSpec((tm, w), lambda i: (i, 0))
    full = lambda a: pl.BlockSpec(a.shape, lambda i: (0,) * a.ndim)
    return pl.pallas_call(
        functools.partial(_k4_body, tm=tm),
        grid=(t // tm,),
        in_specs=[row(D_MODEL), row(HEADS * V_DIM), row(SSM_WIDTH), row(2 * D_MODEL),
                  full(wap), full(wglu), full(wout), full(gffn), full(wrh), full(wrl)],
        out_specs=[row(D_MODEL), pl.BlockSpec((_local_rows(tm), D_MODEL), lambda i: (i, 0)), row(LANES),
                   pl.BlockSpec((SUBLANES, tm), lambda i: (0, i)),
                   pl.BlockSpec((1, 1, LANES), lambda i: (i, 0, 0))],
        out_shape=[jax.ShapeDtypeStruct((t, D_MODEL), F32),
                   jax.ShapeDtypeStruct((t // tm * _local_rows(tm), D_MODEL), F32),
                   jax.ShapeDtypeStruct((t, LANES), F32), jax.ShapeDtypeStruct((SUBLANES, t), F32),
                   jax.ShapeDtypeStruct((t // tm, 1, LANES), F32)],
        compiler_params=pltpu.CompilerParams(dimension_semantics=("arbitrary",),
                                             vmem_limit_bytes=VMEM_LIMIT),
        name="k4_merge_router",
    )(x, o, ys, gates, wap, wglu, wout, gffn, wrh, wrl)


def _row_copy(src, dst, sem):
    return pltpu.make_async_copy(src, dst, sem)


def _token_tile(ref, r):
    return ref.at[pl.ds(pl.multiple_of(r * ROW_BLOCKS, ROW_BLOCKS), ROW_BLOCKS)]


def _expert_body(te_ref, tv_ref, cs_ref, xsl_ref, wg_ref, wu_ref, wd_ref, ys_ref,
                 wgb_ref, wub_ref, wdb_ref, xbuf_ref, sem, *, n_steps):
    i = pl.program_id(0)
    slot = lax.rem(i, 2)

    def fetch(tile, into):
        def start(c2, _):
            for kk in range(2):
                c = 2 * c2 + kk
                chunk = cs_ref[tile * CHUNKS_PER_TILE + c]
                src = xsl_ref.at[pl.ds(pl.multiple_of(chunk * CHUNK_ROWS, CHUNK_ROWS), CHUNK_ROWS)]
                dst = xbuf_ref.at[into, pl.ds(pl.multiple_of(c * CHUNK_ROWS, CHUNK_ROWS), CHUNK_ROWS)]
                _row_copy(src, dst, sem.at[into]).start(priority=kk)
            return 0
        lax.fori_loop(0, CHUNKS_PER_TILE // 2, start, 0, unroll=ROW_UNROLL // 2)

    @pl.when(i == 0)
    def _():
        fetch(0, 0)

    @pl.when((i + 1 < n_steps) & (tv_ref[jnp.minimum(i + 1, n_steps - 1)] == 1))
    def _():
        fetch(i + 1, 1 - slot)

    prev = te_ref[jnp.maximum(i - 1, 0)]

    @pl.when((i == 0) | (te_ref[i] != prev))
    def _():
        wgb_ref[...] = wg_ref[0, 0].astype(BF16)
        wub_ref[...] = wu_ref[0, 0].astype(BF16)
        wdb_ref[...] = wd_ref[0, 0].astype(BF16)

    @pl.when(tv_ref[i] == 1)
    def _():
        _row_copy(xsl_ref.at[pl.ds(0, ROW_TILE)], xbuf_ref.at[slot], sem.at[slot]).wait()
        xb = xbuf_ref[slot].astype(BF16)
        hidden = jax.nn.silu(_dot(xb, wgb_ref[...])) * _dot(xb, wub_ref[...])
        _to_token_tiles(ys_ref, _dot(hidden.astype(BF16), wdb_ref[...]), ROW_TILE)

    @pl.when(tv_ref[i] == 0)
    def _():
        ys_ref[...] = jnp.zeros_like(ys_ref)


def _experts(tile_expert, tile_valid, chunk_src, xsl, wg, wu, wd, layer):
    n_steps = tile_expert.shape[0]
    tile_rows = ROW_TILE * ROW_BLOCKS
    grid_spec = pltpu.PrefetchScalarGridSpec(
        num_scalar_prefetch=3,
        grid=(n_steps,),
        in_specs=[pl.BlockSpec(memory_space=pl.ANY),
                  pl.BlockSpec((1, 1, D_MODEL, D_EXPERT), lambda i, te, tv, cs: (layer, te[i], 0, 0)),
                  pl.BlockSpec((1, 1, D_MODEL, D_EXPERT), lambda i, te, tv, cs: (layer, te[i], 0, 0)),
                  pl.BlockSpec((1, 1, D_EXPERT, D_MODEL), lambda i, te, tv, cs: (layer, te[i], 0, 0))],
        out_specs=pl.BlockSpec((tile_rows, LANES), lambda i, te, tv, cs: (i, 0)),
        scratch_shapes=[pltpu.VMEM((D_MODEL, D_EXPERT), BF16), pltpu.VMEM((D_MODEL, D_EXPERT), BF16),
                        pltpu.VMEM((D_EXPERT, D_MODEL), BF16), pltpu.VMEM((2, ROW_TILE, D_MODEL), F32),
                        pltpu.SemaphoreType.DMA((2,))],
    )
    return pl.pallas_call(
        functools.partial(_expert_body, n_steps=n_steps),
        grid_spec=grid_spec,
        out_shape=jax.ShapeDtypeStruct((n_steps * tile_rows, LANES), F32),
        compiler_params=pltpu.CompilerParams(dimension_semantics=("arbitrary",),
                                             vmem_limit_bytes=VMEM_LIMIT),
        name="moe_experts",
    )(tile_expert, tile_valid, chunk_src, xsl, wg, wu, wd)


def _combine_body(pos_ref, nxt_ref, x_ref, route_ref, ys_ref, out_ref, buf_ref, sem, *, tc, n_steps):
    i = pl.program_id(0)
    slot = lax.rem(i, 2)

    def fetch(rows_of, into):
        def start(r, _):
            for kk in range(2):
                _row_copy(_token_tile(ys_ref, rows_of[0, kk, r]), _token_tile(buf_ref.at[into, kk], r),
                          sem.at[into]).start(priority=kk)
            return 0
        lax.fori_loop(0, tc, start, 0, unroll=ROW_UNROLL)

    @pl.when(i == 0)
    def _():
        fetch(pos_ref, 0)

    @pl.when(i + 1 < n_steps)
    def _():
        fetch(nxt_ref, 1 - slot)

    for kk in range(2):
        _row_copy(ys_ref.at[pl.ds(0, tc * ROW_BLOCKS)], buf_ref.at[slot, kk], sem.at[slot]).wait()
    route = route_ref[...]
    out_ref[...] = (x_ref[...] + route[:, 0:1] * _from_token_tiles(buf_ref.at[slot, 0], tc)
                    + route[:, 1:2] * _from_token_tiles(buf_ref.at[slot, 1], tc))


def _combine(pos, x, route, ys, tc):
    t = x.shape[0]
    n_steps = t // tc
    return pl.pallas_call(
        functools.partial(_combine_body, tc=tc, n_steps=n_steps),
        grid=(n_steps,),
        in_specs=[pl.BlockSpec((1, 2, tc), lambda i: (i, 0, 0), memory_space=pltpu.SMEM),
                  pl.BlockSpec((1, 2, tc), lambda i: (i, 0, 0), memory_space=pltpu.SMEM),
                  pl.BlockSpec((tc, D_MODEL), lambda i: (i, 0)),
                  pl.BlockSpec((tc, LANES), lambda i: (i, 0)),
                  pl.BlockSpec(memory_space=pl.ANY)],
        out_specs=pl.BlockSpec((tc, D_MODEL), lambda i: (i, 0)),
        out_shape=jax.ShapeDtypeStruct((t, D_MODEL), F32),
        scratch_shapes=[pltpu.VMEM((2, 2, tc * ROW_BLOCKS, LANES), F32), pltpu.SemaphoreType.DMA((2,))],
        compiler_params=pltpu.CompilerParams(dimension_semantics=("arbitrary",)),
        name="moe_combine",
    )(pos, jnp.concatenate([pos[1:], pos[-1:]], axis=0), x, route, ys)


def _expert_tiles(t, tm):
    return (2 * t + (t // tm) * N_EXPERTS * (CHUNK_ROWS - 1)) // ROW_TILE + N_EXPERTS


def _routing_plan(route_t, counts, tm):
    i32 = jnp.int32
    t = route_t.shape[1]
    n_tok_tiles = t // tm
    local_chunks = _local_rows(tm) // CHUNK_ROWS
    n_tiles = _expert_tiles(t, tm)
    cnt = counts[:, 0, N_GROUPS:N_GROUPS + N_EXPERTS].astype(i32)
    chunks = (cnt + CHUNK_ROWS - 1) // CHUNK_ROWS
    first_local = jnp.cumsum(chunks, axis=1) - chunks
    upto = jnp.cumsum(chunks, axis=0)
    before = upto - chunks
    n_chunks = upto[-1]
    n_exp_tiles = (n_chunks + CHUNKS_PER_TILE - 1) // CHUNKS_PER_TILE
    tile_end = jnp.cumsum(n_exp_tiles)
    tile_start = tile_end - n_exp_tiles
    used = tile_end[-1]
    tiles = jnp.arange(n_tiles, dtype=i32)
    valid = tiles < used
    last = jnp.maximum(used - 1, 0)
    expert_of = jnp.sum((tile_end[None, :] <= tiles[:, None]).astype(i32), axis=1)
    tile_expert = jnp.where(valid, expert_of, jnp.sum(jnp.where(tiles == last, expert_of, 0)))

    sel = tile_expert[:, None] == jnp.arange(N_EXPERTS, dtype=i32)[None, :]
    per_tile = lambda table: jnp.sum(jnp.where(sel[:, :, None], table.T[None], 0), axis=1)
    per_tile1 = lambda vec: jnp.sum(jnp.where(sel, vec[None, :], 0), axis=1)
    upto_g, before_g, first_g = per_tile(upto), per_tile(before), per_tile(first_local)
    k = ((tiles - per_tile1(tile_start))[:, None] * CHUNKS_PER_TILE
         + jnp.arange(CHUNKS_PER_TILE, dtype=i32)[None, :])
    k_valid = valid[:, None] & (k < per_tile1(n_chunks)[:, None])
    src_tile = jnp.minimum(jnp.sum((upto_g[:, None, :] <= k[:, :, None]).astype(i32), axis=2), n_tok_tiles - 1)
    at = src_tile[:, :, None] == jnp.arange(n_tok_tiles, dtype=i32)[None, None, :]
    before_at = jnp.sum(jnp.where(at, before_g[:, None, :], 0), axis=2)
    first_at = jnp.sum(jnp.where(at, first_g[:, None, :], 0), axis=2)
    chunk = src_tile * local_chunks + first_at + (k - before_at)
    chunk_src = jnp.where(k_valid, chunk, local_chunks - 1).reshape(-1).astype(i32)

    base = tile_start[None, :] * ROW_TILE + CHUNK_ROWS * (before - first_local)
    base_t = jnp.repeat(base.T, tm, axis=1)
    experts = jnp.arange(N_EXPERTS, dtype=i32)[:, None]

    def position(e_row, local_row):
        e = e_row.astype(i32)[None, :]
        return jnp.sum(jnp.where(e == experts, base_t, 0), axis=0) + local_row.astype(i32)

    pos = jnp.stack([position(route_t[2], route_t[4]).reshape(n_tok_tiles, tm),
                     position(route_t[3], route_t[5]).reshape(n_tok_tiles, tm)], axis=1)
    return pos, tile_expert, valid.astype(i32), chunk_src


def _rope_tables(positions):
    half = ROPE // 2
    inv_freq = ROPE_THETA ** (-jnp.arange(half, dtype=F32) / half)
    ang = inv_freq[:, None] * positions.astype(F32).reshape(1, -1)
    spread = lambda a: jnp.tile(a.T, (1, LANES // half))
    return spread(jnp.cos(ang)), spread(jnp.sin(ang))


def _head_slots(w, width):
    k = w.shape[0]
    w = w.reshape(k, HEADS, width)
    return jnp.pad(w, ((0, 0), (0, 0), (0, LANES - width))).reshape(k, HEADS * LANES)


def _layer_weights(wi, w_uq, w_ukv, q_head_g, k_head_g, w_router_group, w_router_expert,
                   w_attn_proj, w_glu, w_out):
    kr = jnp.pad(wi[:, C_KR:C_KR + ROPE], ((0, 0), (NOPE, LANES - QK_DIM)))
    win = jnp.concatenate([wi[:, :C_KR], kr, wi[:, C_KR + ROPE:]], axis=1).astype(BF16)
    wuq = _head_slots(w_uq, QK_DIM).astype(BF16)
    kv = w_ukv.reshape(KV_LORA, HEADS, NOPE + V_DIM)
    wuk = _head_slots(kv[:, :, :NOPE].reshape(KV_LORA, HEADS * NOPE), NOPE)
    wuv = _head_slots(kv[:, :, NOPE:].reshape(KV_LORA, HEADS * V_DIM), V_DIM)
    wukv = jnp.concatenate([wuk, wuv], axis=1).astype(BF16)
    pad_g = lambda g: jnp.tile(jnp.pad(g.astype(F32), (0, LANES - QK_DIM)), HEADS)[None, :]
    gq = pad_g(q_head_g) * (QK_DIM ** -0.5 * math.log2(math.e))
    gk = pad_g(k_head_g)
    wr = jnp.concatenate([w_router_group, w_router_expert], axis=1).astype(F32)
    wr = jnp.pad(wr, ((0, 0), (0, LANES - wr.shape[1])))
    wrh = wr.astype(BF16)
    wrl = (wr - wrh.astype(F32)).astype(BF16)
    return (win, wuq, wukv, gq, gk, wrh, wrl,
            w_attn_proj.astype(BF16), w_glu.astype(BF16), w_out.astype(BF16))


def kernel(x, positions, norm_mix_g, w_in, q_lora_g, w_uq, kv_lora_g, w_ukv, q_head_g, k_head_g, w_attn_proj, lam_re, lam_im, log_dt, b_re, b_im, c_re, c_im, d_skip, w_glu, w_out, norm_ffn_g, w_router_group, w_router_expert, w_exp_gate, w_exp_up, w_exp_down):
    bsz, seq, _ = x.shape
    t = bsz * seq
    depth = w_in.shape[0]
    tm = 512 if t % 512 == 0 else 256
    tq = 256
    nsub = 2 if seq % (2 * tq) == 0 else 1
    bb = 4 if bsz % 4 == 0 else (2 if bsz % 2 == 0 else 1)
    row1 = lambda g: g.astype(F32)[None, :]

    cos_t, sin_t = _rope_tables(positions)
    xf = x.reshape(t, D_MODEL).astype(F32)
    for l in range(depth):
        win, wuq, wukv, gq, gk, wrh, wrl, wap, wglu, wout = _layer_weights(
            w_in[l], w_uq[l], w_ukv[l], q_head_g[l], k_head_g[l], w_router_group[l], w_router_expert[l],
            w_attn_proj[l], w_glu[l], w_out[l])
        tables = _ssm_tables(lam_re[l], lam_im[l], log_dt[l], b_re[l], b_im[l], c_re[l], c_im[l], d_skip[l])
        q, k, v, u, gates = _k1(xf, row1(norm_mix_g[l]), win, row1(q_lora_g[l]), wuq,
                                row1(kv_lora_g[l]), wukv, gq, gk, cos_t, sin_t, tm)
        o = _attention(q.reshape(bsz, seq, -1), k.reshape(bsz, seq, -1), v.reshape(bsz, seq, -1), tq, nsub)
        ys = _ssm(u.reshape(bsz, seq, SSM_WIDTH), *tables, bb)
        xn, xsl, route, route_t, counts = _k4(xf, o.reshape(t, -1), ys.reshape(t, SSM_WIDTH), gates,
                                              wap, wglu, wout, row1(norm_ffn_g[l]), wrh, wrl, tm)
        pos, tile_expert, tile_valid, chunk_src = _routing_plan(route_t, counts, tm)
        ye = _experts(tile_expert, tile_valid, chunk_src, xsl, w_exp_gate, w_exp_up, w_exp_down, l)
        xf = _combine(pos, xn, route, ye, tm)
    return xf.reshape(bsz, seq, D_MODEL).astype(x.dtype)
```

```python
import functools
import math

import jax
import jax.numpy as jnp
from jax import lax
from jax.experimental import pallas as pl
from jax.experimental.pallas import tpu as pltpu

F32 = jnp.float32
BF16 = jnp.bfloat16

D_MODEL = 1024
CHUNK = 64
HEADS = 8
NOPE = 64
ROPE = 32
QK_DIM = NOPE + ROPE
V_DIM = 64
Q_LORA = 384
KV_LORA = 256
ROPE_THETA = 10000.0
SSM_WIDTH = 512
SSM_GROUP = 16
SSM_GROUPS = 32
SSM_STATE = 64
N_GROUPS = 4
PER_GROUP = 8
N_EXPERTS = 32
D_EXPERT = 256
EPS = 1e-6

LANES = 128
SUBLANES = 8
SSM_CHUNK = 16
N_PAIR = SSM_CHUNK // 2
SSM_QBLOCKS = SSM_WIDTH // LANES
GROUPS_PER_QBLOCK = LANES // SSM_GROUP
STATE_COLS = GROUPS_PER_QBLOCK * 2 * SSM_STATE
ROW_TILE = 512
ROW_UNROLL = 8
K1_SLABS = 1
K4_SLABS = 2
ROW_BLOCKS = D_MODEL // LANES
CHUNK_ROWS = SUBLANES
CHUNKS_PER_TILE = ROW_TILE // CHUNK_ROWS
EXPERT_AHEAD = 2
VMEM_LIMIT = 56 * 1024 * 1024


def _local_rows(tm):
    return 2 * tm + N_EXPERTS * CHUNK_ROWS

C_Q = 0
C_KV = C_Q + Q_LORA
C_KR = C_KV + KV_LORA
C_U = C_KR + LANES
C_GATE = C_U + SSM_WIDTH
IN_COLS_PAD = C_GATE + 2 * D_MODEL


def _rms(x, g):
    return x * lax.rsqrt(jnp.mean(x * x, axis=-1, keepdims=True) + EPS) * g


def _dot(a, b):
    return jnp.dot(a, b, preferred_element_type=F32)


def _to_token_tiles(ref, x, n, base=0):
    for k in range(ROW_BLOCKS):
        ref[pl.ds(base * ROW_BLOCKS + k, n, stride=ROW_BLOCKS), :] = x[:, k * LANES:(k + 1) * LANES]


def _from_token_tiles(ref, n):
    return jnp.concatenate([ref[pl.ds(k, n, stride=ROW_BLOCKS), :] for k in range(ROW_BLOCKS)], axis=1)


def _k1_body(x_ref, gmix_ref, win_ref, gql_ref, wuq_ref, gkvl_ref, wukv_ref, gq_ref, gk_ref,
             cos_ref, sin_ref, q_ref, k_ref, v_ref, u_ref, gates_ref):
    vlane = lax.broadcasted_iota(jnp.int32, (1, HEADS * LANES), 1)
    ones_col = ((vlane % LANES) == V_DIM).astype(F32)
    ts = x_ref.shape[0] // K1_SLABS
    for s in range(K1_SLABS):
        rows = pl.ds(s * ts, ts)
        hb = _rms(x_ref[rows, :], gmix_ref[...]).astype(BF16)

        def proj(lo, hi):
            return _dot(hb, win_ref[:, lo:hi])

        lane = lax.broadcasted_iota(jnp.int32, (ts, LANES), 1)
        sin = sin_ref[rows, :]
        cos_t = jnp.where(lane < NOPE, 1.0, jnp.where(lane < QK_DIM, cos_ref[rows, :], 0.0))
        sin_m = jnp.where((lane >= NOPE) & (lane < NOPE + ROPE // 2), -sin, 0.0)
        sin_p = jnp.where((lane >= NOPE + ROPE // 2) & (lane < QK_DIM), sin, 0.0)

        def head_norm_rope(r, g):
            ms = jnp.sum(r * r, axis=-1, keepdims=True) * (1.0 / QK_DIM)
            rn = r * lax.rsqrt(ms + EPS) * g
            return (rn * cos_t + pltpu.roll(rn, LANES - ROPE // 2, 1) * sin_m
                    + pltpu.roll(rn, ROPE // 2, 1) * sin_p)

        qn = _rms(proj(C_Q, C_KV), gql_ref[...]).astype(BF16)
        qf = _dot(qn, wuq_ref[...])
        for h in range(HEADS):
            sl = slice(h * LANES, (h + 1) * LANES)
            q_ref[rows, sl] = head_norm_rope(qf[:, sl], gq_ref[:, sl]).astype(BF16)

        kvn = _rms(proj(C_KV, C_KR), gkvl_ref[...]).astype(BF16)
        kf = _dot(kvn, wukv_ref[...])
        kr = proj(C_KR, C_U)
        for h in range(HEADS):
            sl = slice(h * LANES, (h + 1) * LANES)
            k_ref[rows, sl] = head_norm_rope(kf[:, sl] + kr, gk_ref[:, sl]).astype(BF16)
        v_ref[rows, :] = (kf[:, HEADS * LANES:] + ones_col).astype(BF16)

        u_ref[rows, :] = proj(C_U, C_GATE)
        gates_ref[rows, :] = jax.nn.sigmoid(proj(C_GATE, IN_COLS_PAD)).astype(BF16)


def _k1(x, gmix, win, gql, wuq, gkvl, wukv, gq, gk, cos_t, sin_t, tm):
    t = x.shape[0]
    row = lambda w: pl.BlockSpec((tm, w), lambda i: (i, 0))
    full = lambda a: pl.BlockSpec(a.shape, lambda i: (0,) * a.ndim)
    return pl.pallas_call(
        _k1_body,
        grid=(t // tm,),
        in_specs=[row(D_MODEL), full(gmix), full(win), full(gql), full(wuq), full(gkvl), full(wukv),
                  full(gq), full(gk), row(LANES), row(LANES)],
        out_specs=[row(HEADS * LANES), row(HEADS * LANES), row(HEADS * LANES), row(SSM_WIDTH),
                   row(2 * D_MODEL)],
        out_shape=[jax.ShapeDtypeStruct((t, HEADS * LANES), BF16),
                   jax.ShapeDtypeStruct((t, HEADS * LANES), BF16),
                   jax.ShapeDtypeStruct((t, HEADS * LANES), BF16),
                   jax.ShapeDtypeStruct((t, SSM_WIDTH), F32),
                   jax.ShapeDtypeStruct((t, 2 * D_MODEL), BF16)],
        compiler_params=pltpu.CompilerParams(dimension_semantics=("arbitrary",),
                                             vmem_limit_bytes=VMEM_LIMIT),
        name="k1_inproj",
    )(x, gmix, win, gql, wuq, gkvl, wukv, gq, gk, cos_t, sin_t)


def _attn_body(q_ref, k_ref, v_ref, o_ref, *, tq, nsub, nq):
    i = pl.program_id(1)
    row_chunk = lax.broadcasted_iota(jnp.int32, (tq, tq), 0) // CHUNK
    col_chunk = lax.broadcasted_iota(jnp.int32, (tq, tq), 1) // CHUNK
    diag_mask = col_chunk <= row_chunk
    neg = jnp.float32(-1e30)

    def scores(rows, hh, n_full):
        sl = slice(hh * LANES, (hh + 1) * LANES)
        s = lax.dot_general(q_ref[0, rows, sl], k_ref[0, :n_full + tq, sl], (((1,), (1,)), ((), ())),
                            preferred_element_type=F32)
        s_diag = jnp.where(diag_mask, s[:, n_full:], neg)
        return jnp.concatenate([s[:, :n_full], s_diag], axis=1) if n_full else s_diag

    def probs(s):
        return jnp.exp2(s - jnp.max(s, axis=-1, keepdims=True)).astype(BF16)

    def values(p, hh, n_full):
        of = _dot(p, v_ref[0, :n_full + tq, hh * LANES:(hh + 1) * LANES])
        return of * (1.0 / of[:, V_DIM:V_DIM + 1])

    for ii in range(nq):
        @pl.when(i == ii)
        def _(ii=ii):
            lane = lax.broadcasted_iota(jnp.int32, (tq, LANES), 1)
            work = [(pl.ds(sub * tq, tq), (ii * nsub + sub) * tq) for sub in range(nsub)]
            ss = [[scores(rows, hh, n_full) for hh in range(HEADS)] for rows, n_full in work]
            ps = [[probs(s) for s in per_block] for per_block in ss]
            for (rows, n_full), per_block in zip(work, ps):
                os_ = [values(p, hh, n_full) for hh, p in enumerate(per_block)]
                for hp in range(HEADS // 2):
                    o_ref[0, rows, hp * LANES:(hp + 1) * LANES] = jnp.where(
                        lane < V_DIM, os_[2 * hp], pltpu.roll(os_[2 * hp + 1], V_DIM, 1)).astype(BF16)


def _attention(q, k, v, tq, nsub):
    b, l, _ = q.shape
    nq = l // (tq * nsub)
    return pl.pallas_call(
        functools.partial(_attn_body, tq=tq, nsub=nsub, nq=nq),
        grid=(b, nq),
        in_specs=[pl.BlockSpec((1, tq * nsub, HEADS * LANES), lambda bi, i: (bi, i, 0)),
                  pl.BlockSpec((1, l, HEADS * LANES), lambda bi, i: (bi, 0, 0)),
                  pl.BlockSpec((1, l, HEADS * LANES), lambda bi, i: (bi, 0, 0))],
        out_specs=pl.BlockSpec((1, tq * nsub, HEADS * V_DIM), lambda bi, i: (bi, i, 0)),
        out_shape=jax.ShapeDtypeStruct((b, l, HEADS * V_DIM), BF16),
        compiler_params=pltpu.CompilerParams(
            dimension_semantics=("arbitrary", "arbitrary"), vmem_limit_bytes=VMEM_LIMIT),
        name="attention",
    )(q, k, v)


def _cmul(a, b):
    return a[0] * b[0] - a[1] * b[1], a[0] * b[1] + a[1] * b[0]


def _ssm_tables(lam_re, lam_im, log_dt, b_re, b_im, c_re, c_im, d_skip):
    lc = SSM_CHUNK
    nq, gq = SSM_QBLOCKS, GROUPS_PER_QBLOCK
    lr, li = lam_re.astype(F32), lam_im.astype(F32)
    dt = jnp.exp(log_dt.astype(F32))[:, None]
    steps = jnp.arange(lc + 1, dtype=F32)[:, None, None]
    mag = jnp.exp(lr * dt * steps)
    pw = (mag * jnp.cos(li * dt * steps), mag * jnp.sin(li * dt * steps))
    num = (pw[0][1] - 1.0, pw[1][1])
    den = lr * lr + li * li
    ratio = ((num[0] * lr + num[1] * li) / den, (num[1] * lr - num[0] * li) / den)
    b_bar = _cmul((ratio[0][..., None], ratio[1][..., None]), (b_re.astype(F32), b_im.astype(F32)))
    c = (c_re.astype(F32), c_im.astype(F32))

    m = _cmul((pw[0][:lc, :, :, None], pw[1][:lc, :, :, None]), (b_bar[0][None], b_bar[1][None]))
    kern = jnp.einsum('gip,dgpj->dgij', c[0], m[0]) - jnp.einsum('gip,dgpj->dgij', c[1], m[1])
    kern = kern.at[0].add(jnp.eye(SSM_GROUP, dtype=F32)[None] * d_skip.astype(F32)[:, :, None])
    kpad = jnp.concatenate([jnp.zeros_like(kern[:1]), kern], axis=0)
    big_d = jnp.arange(N_PAIR)
    rows = []
    for s2 in range(2):
        cols = []
        for t2 in range(2):
            kd = kpad[2 * big_d + t2 - s2 + 1].reshape(N_PAIR, nq, gq, SSM_GROUP, SSM_GROUP)
            cols.append(jnp.transpose(kd, (1, 0, 4, 2, 3)))
        rows.append(jnp.stack(cols, axis=3))
    tab_t = jnp.stack(rows, axis=2).reshape(nq, N_PAIR, 2, SSM_GROUP, 2 * LANES)

    bt = (jnp.swapaxes(b_bar[0], 1, 2), jnp.swapaxes(b_bar[1], 1, 2))
    vin = _cmul((pw[0][:lc][::-1][:, :, None, :], pw[1][:lc][::-1][:, :, None, :]), (bt[0][None], bt[1][None]))
    vin = jnp.concatenate(vin, axis=-1).reshape(N_PAIR, 2, nq, gq, SSM_GROUP, 2 * SSM_STATE)
    tab_in = jnp.transpose(vin, (2, 0, 1, 4, 3, 5)).reshape(nq, N_PAIR, 2, SSM_GROUP, STATE_COLS)

    cw = _cmul((c[0][None], c[1][None]), (pw[0][1:lc + 1][:, :, None, :], pw[1][1:lc + 1][:, :, None, :]))
    cw = jnp.concatenate([cw[0], -cw[1]], axis=-1).reshape(N_PAIR, 2, nq, gq, SSM_GROUP, 2 * SSM_STATE)
    tab_out = jnp.transpose(cw, (2, 0, 1, 3, 4, 5)).reshape(nq, N_PAIR, 2 * LANES, 2 * SSM_STATE)

    ar = pw[0][lc].reshape(nq, gq, 1, SSM_STATE)
    ai = pw[1][lc].reshape(nq, gq, 1, SSM_STATE)
    a_same = jnp.concatenate([ar, ar], axis=2).reshape(nq, 1, STATE_COLS)
    a_swap = jnp.concatenate([-ai, ai], axis=2).reshape(nq, 1, STATE_COLS)
    return tab_t, tab_in, tab_out, jnp.concatenate([a_same, a_swap], axis=1)


def _ssm_body(u_ref, tt_ref, tin_ref, tout_ref, a_ref, y_ref, wt_ref, win_ref, wout_ref, st_ref, sw_ref,
              hp_ref, *, bb, nc):
    gq = GROUPS_PER_QBLOCK

    @pl.when(pl.program_id(1) == 0)
    def _():
        grp_t = (lax.broadcasted_iota(jnp.int32, (SSM_GROUP, 2 * LANES), 1) % LANES) // SSM_GROUP
        grp_in = lax.broadcasted_iota(jnp.int32, (SSM_GROUP, STATE_COLS), 1) // LANES
        grp_out = (lax.broadcasted_iota(jnp.int32, (LANES, 2 * LANES), 1) % LANES) // SSM_GROUP
        for d in range(N_PAIR):
            for s2 in range(2):
                piece_t = tt_ref[0, d, s2]
                piece_in = tin_ref[0, d, s2]
                for gl in range(gq):
                    rows = pl.ds(s2 * LANES + gl * SSM_GROUP, SSM_GROUP)
                    wt_ref[d, rows, :] = jnp.where(grp_t == gl, piece_t, 0.0).astype(BF16)
                    win_ref[d, rows, :] = jnp.where(grp_in == gl, piece_in, 0.0).astype(BF16)
            piece_out = tout_ref[0, d].T
            for gl in range(gq):
                wout_ref[d, pl.ds(gl * LANES, LANES), :] = jnp.where(grp_out == gl, piece_out, 0.0).astype(BF16)

    xs = []
    for s in range(N_PAIR):
        per_batch = []
        for b in range(bb):
            x0 = u_ref[b, pl.ds(2 * s, nc, stride=SSM_CHUNK), :]
            x1 = u_ref[b, pl.ds(2 * s + 1, nc, stride=SSM_CHUNK), :]
            per_batch.append(jnp.concatenate([x0, x1], axis=1))
        xs.append(jnp.concatenate(per_batch, axis=0).astype(BF16))

    st = _dot(xs[0], win_ref[0])
    for s in range(1, N_PAIR):
        st = st + _dot(xs[s], win_ref[s])
    st_ref[...] = st
    sw_ref[...] = jnp.concatenate(
        [pltpu.roll(st[:, g * LANES:(g + 1) * LANES], SSM_STATE, 1) for g in range(gq)], axis=1)

    a_same = a_ref[0, 0:1, :]
    a_swap = a_ref[0, 1:2, :]

    def step(c, carry):
        new = []
        for b in range(bb):
            h, hs = carry[b]
            row = b * nc + c
            hp_ref[pl.ds(row, 1), :] = h
            new.append((a_same * h + a_swap * hs + st_ref[pl.ds(row, 1), :],
                        a_same * hs - a_swap * h + sw_ref[pl.ds(row, 1), :]))
        return tuple(new)

    zero = jnp.zeros((1, STATE_COLS), F32)
    lax.fori_loop(0, nc, step, tuple((zero, zero) for _ in range(bb)))

    hp = hp_ref[...].astype(BF16)
    for t in range(N_PAIR):
        acc = _dot(hp, wout_ref[t])
        for s in range(t + 1):
            acc = acc + _dot(xs[s], wt_ref[t - s])
        yg = jax.nn.gelu(acc)
        for b in range(bb):
            rows = slice(b * nc, (b + 1) * nc)
            y_ref[b, pl.ds(2 * t, nc, stride=SSM_CHUNK), :] = yg[rows, :LANES]
            y_ref[b, pl.ds(2 * t + 1, nc, stride=SSM_CHUNK), :] = yg[rows, LANES:]


def _ssm(u, tab_t, tab_in, tab_out, a_chunk, bb):
    b, l, _ = u.shape
    nc = l // SSM_CHUNK
    wspec = lambda a: pl.BlockSpec((1,) + a.shape[1:], lambda q, bi: (q,) + (0,) * (a.ndim - 1))
    return pl.pallas_call(
        functools.partial(_ssm_body, bb=bb, nc=nc),
        grid=(SSM_QBLOCKS, b // bb),
        in_specs=[pl.BlockSpec((bb, l, LANES), lambda q, bi: (bi, 0, q)),
                  wspec(tab_t), wspec(tab_in), wspec(tab_out), wspec(a_chunk)],
        out_specs=pl.BlockSpec((bb, l, LANES), lambda q, bi: (bi, 0, q)),
        out_shape=jax.ShapeDtypeStruct((b, l, SSM_WIDTH), F32),
        scratch_shapes=[pltpu.VMEM((N_PAIR, 2 * LANES, 2 * LANES), BF16),
                        pltpu.VMEM((N_PAIR, 2 * LANES, STATE_COLS), BF16),
                        pltpu.VMEM((N_PAIR, STATE_COLS, 2 * LANES), BF16),
                        pltpu.VMEM((bb * nc, STATE_COLS), F32),
                        pltpu.VMEM((bb * nc, STATE_COLS), F32),
                        pltpu.VMEM((bb * nc, STATE_COLS), F32)],
        compiler_params=pltpu.CompilerParams(dimension_semantics=("arbitrary", "arbitrary"),
                                             vmem_limit_bytes=VMEM_LIMIT),
        name="ssm",
    )(u, tab_t, tab_in, tab_out, a_chunk)


def _k4_body(x_ref, o_ref, ys_ref, g_ref, wap_ref, wglu_ref, wout_ref, gffn_ref, wrh_ref, wrl_ref,
             xn_ref, xsl_ref, route_ref, routet_ref, cnt_ref, *, tm):
    ts = tm // K4_SLABS
    slabs = [pl.ds(s * ts, ts) for s in range(K4_SLABS)]

    logits_all = []
    h2b_all = []
    for s, rows in enumerate(slabs):
        y_attn = _dot(o_ref[rows, :], wap_ref[...])
        vg = _dot(ys_ref[rows, :].astype(BF16), wglu_ref[...])
        y_ssm = vg[:, :D_MODEL] * jax.nn.sigmoid(vg[:, D_MODEL:])
        g = g_ref[rows, :].astype(F32)
        merged = g[:, :D_MODEL] * y_attn + g[:, D_MODEL:] * y_ssm
        xn = x_ref[rows, :] + _dot(merged.astype(BF16), wout_ref[...])
        xn_ref[rows, :] = xn
        h2 = _rms(xn, gffn_ref[...])
        hh = h2.astype(BF16)
        hl = (h2 - hh.astype(F32)).astype(BF16)
        h2b_all.append(hh)
        logits_all.append(_dot(hh, wrh_ref[...]) + _dot(hh, wrl_ref[...]) + _dot(hl, wrh_ref[...]))

    lane = lax.broadcasted_iota(jnp.int32, (ts, LANES), 1).astype(F32)
    ninf = jnp.float32(-jnp.inf)
    big = jnp.float32(4 * LANES)
    tri = (lax.broadcasted_iota(jnp.int32, (ts, ts), 1)
           < lax.broadcasted_iota(jnp.int32, (ts, ts), 0)).astype(BF16)
    seen = jnp.zeros((1, LANES), F32)
    picks = []
    for s, rows in enumerate(slabs):
        logits = logits_all[s]

        def top(mask):
            val = jnp.max(jnp.where(mask, logits, ninf), axis=-1, keepdims=True)
            idx = jnp.min(jnp.where(mask & (logits == val), lane, big), axis=-1, keepdims=True)
            return val, idx

        gmask = lane < N_GROUPS
        gmax, gidx = top(gmask)
        g_w = 1.0 / jnp.sum(jnp.where(gmask, jnp.exp(logits - gmax), 0.0), axis=-1, keepdims=True)
        lo = N_GROUPS + PER_GROUP * gidx
        emask = (lane >= lo) & (lane < lo + PER_GROUP)
        l1, i1 = top(emask)
        l2, i2 = top(emask & (lane != i1))
        e21 = jnp.exp(l2 - l1)
        w1 = g_w / (1.0 + e21)
        w2 = g_w * e21 / (1.0 + e21)

        hot1 = lane == i1
        hot2 = lane == i2
        onehot = (hot1 | hot2).astype(BF16)
        picks.append((w1, w2, i1, i2, hot1, hot2, _dot(tri, onehot) + seen))
        seen = seen + jnp.sum(onehot.astype(F32), axis=0, keepdims=True)
    cnt_ref[0] = seen

    padded = jnp.floor((seen + (CHUNK_ROWS - 1)) * (1.0 / CHUNK_ROWS)) * CHUNK_ROWS
    earlier = (lax.broadcasted_iota(jnp.int32, (LANES, LANES), 0)
               < lax.broadcasted_iota(jnp.int32, (LANES, LANES), 1)).astype(BF16)
    start = _dot(jnp.broadcast_to(padded, (SUBLANES, LANES)).astype(BF16), earlier)[0:1, :]

    local_rows = []
    for s, rows in enumerate(slabs):
        w1, w2, i1, i2, hot1, hot2, before = picks[s]
        r1 = jnp.sum(jnp.where(hot1, before + start, 0.0), axis=-1, keepdims=True)
        r2 = jnp.sum(jnp.where(hot2, before + start, 0.0), axis=-1, keepdims=True)
        cols = (w1, w2, i1 - N_GROUPS, i2 - N_GROUPS, r1, r2)
        out = jnp.zeros((ts, LANES), F32)
        for n, col in enumerate(cols):
            out = jnp.where(lane == n, col, out)
        route_ref[rows, :] = out
        out_t = out.T[:SUBLANES, :]
        routet_ref[:, rows] = out_t
        local_rows.append(out_t[4:6, :])

    lr = jnp.concatenate(local_rows, axis=1)
    n_local = xsl_ref.shape[0]
    row_id = lax.broadcasted_iota(jnp.int32, (n_local, tm), 0).astype(F32)
    select = ((row_id == lr[0:1, :]) | (row_id == lr[1:2, :])).astype(BF16)
    xsl_ref[...] = _dot(select, jnp.concatenate(h2b_all, axis=0))


def _k4(x, o, ys, gates, wap, wglu, wout, gffn, wrh, wrl, tm):
    t = x.shape[0]
    row = lambda w: pl.BlockSpec((tm, w), lambda i: (i, 0))
    full = lambda a: pl.BlockSpec(a.shape, lambda i: (0,) * a.ndim)
    return pl.pallas_call(
        functools.partial(_k4_body, tm=tm),
        grid=(t // tm,),
        in_specs=[row(D_MODEL), row(HEADS * V_DIM), row(SSM_WIDTH), row(2 * D_MODEL),
                  full(wap), full(wglu), full(wout), full(gffn), full(wrh), full(wrl)],
        out_specs=[row(D_MODEL), pl.BlockSpec((_local_rows(tm), D_MODEL), lambda i: (i, 0)), row(LANES),
                   pl.BlockSpec((SUBLANES, tm), lambda i: (0, i)),
                   pl.BlockSpec((1, 1, LANES), lambda i: (i, 0, 0))],
        out_shape=[jax.ShapeDtypeStruct((t, D_MODEL), F32),
                   jax.ShapeDtypeStruct((t // tm * _local_rows(tm), D_MODEL), F32),
                   jax.ShapeDtypeStruct((t, LANES), F32), jax.ShapeDtypeStruct((SUBLANES, t), F32),
                   jax.ShapeDtypeStruct((t // tm, 1, LANES), F32)],
        compiler_params=pltpu.CompilerParams(dimension_semantics=("arbitrary",),
                                             vmem_limit_bytes=VMEM_LIMIT),
        name="k4_merge_router",
    )(x, o, ys, gates, wap, wglu, wout, gffn, wrh, wrl)


def _row_copy(src, dst, sem):
    return pltpu.make_async_copy(src, dst, sem)


def _token_tile(ref, r):
    return ref.at[pl.ds(pl.multiple_of(r * ROW_BLOCKS, ROW_BLOCKS), ROW_BLOCKS)]


def _expert_body(te_ref, tv_ref, cs_ref, xsl_ref, wg_ref, wu_ref, wd_ref, ys_ref,
                 wgb_ref, wub_ref, wdb_ref, xbuf_ref, sem, *, n_steps):
    i = pl.program_id(0)
    n_buf = EXPERT_AHEAD + 1
    slot = lax.rem(i, n_buf)

    def fetch(tile, into):
        def start(c2, _):
            for kk in range(2):
                c = 2 * c2 + kk
                chunk = cs_ref[tile * CHUNKS_PER_TILE + c]
                src = xsl_ref.at[pl.ds(pl.multiple_of(chunk * CHUNK_ROWS, CHUNK_ROWS), CHUNK_ROWS)]
                dst = xbuf_ref.at[into, pl.ds(pl.multiple_of(c * CHUNK_ROWS, CHUNK_ROWS), CHUNK_ROWS)]
                _row_copy(src, dst, sem.at[into]).start(priority=kk)
            return 0
        lax.fori_loop(0, CHUNKS_PER_TILE // 2, start, 0, unroll=ROW_UNROLL // 2)

    def wanted(tile):
        return (tile < n_steps) & (tv_ref[jnp.minimum(tile, n_steps - 1)] == 1)

    for first in range(EXPERT_AHEAD):
        @pl.when((i == 0) & wanted(first))
        def _(first=first):
            fetch(first, first)

    @pl.when(wanted(i + EXPERT_AHEAD))
    def _():
        fetch(i + EXPERT_AHEAD, lax.rem(i + EXPERT_AHEAD, n_buf))

    prev = te_ref[jnp.maximum(i - 1, 0)]

    @pl.when((i == 0) | (te_ref[i] != prev))
    def _():
        wgb_ref[...] = wg_ref[0, 0].astype(BF16)
        wub_ref[...] = wu_ref[0, 0].astype(BF16)
        wdb_ref[...] = wd_ref[0, 0].astype(BF16)

    @pl.when(tv_ref[i] == 1)
    def _():
        _row_copy(xsl_ref.at[pl.ds(0, ROW_TILE)], xbuf_ref.at[slot], sem.at[slot]).wait()
        xb = xbuf_ref[slot].astype(BF16)
        hidden = jax.nn.silu(_dot(xb, wgb_ref[...])) * _dot(xb, wub_ref[...])
        _to_token_tiles(ys_ref, _dot(hidden.astype(BF16), wdb_ref[...]), ROW_TILE)

    @pl.when(tv_ref[i] == 0)
    def _():
        ys_ref[...] = jnp.zeros_like(ys_ref)


def _experts(tile_expert, tile_valid, chunk_src, xsl, wg, wu, wd, layer):
    n_steps = tile_expert.shape[0]
    tile_rows = ROW_TILE * ROW_BLOCKS
    grid_spec = pltpu.PrefetchScalarGridSpec(
        num_scalar_prefetch=3,
        grid=(n_steps,),
        in_specs=[pl.BlockSpec(memory_space=pl.ANY),
                  pl.BlockSpec((1, 1, D_MODEL, D_EXPERT), lambda i, te, tv, cs: (layer, te[i], 0, 0)),
                  pl.BlockSpec((1, 1, D_MODEL, D_EXPERT), lambda i, te, tv, cs: (layer, te[i], 0, 0)),
                  pl.BlockSpec((1, 1, D_EXPERT, D_MODEL), lambda i, te, tv, cs: (layer, te[i], 0, 0))],
        out_specs=pl.BlockSpec((tile_rows, LANES), lambda i, te, tv, cs: (i, 0)),
        scratch_shapes=[pltpu.VMEM((D_MODEL, D_EXPERT), BF16), pltpu.VMEM((D_MODEL, D_EXPERT), BF16),
                        pltpu.VMEM((D_EXPERT, D_MODEL), BF16),
                        pltpu.VMEM((EXPERT_AHEAD + 1, ROW_TILE, D_MODEL), F32),
                        pltpu.SemaphoreType.DMA((EXPERT_AHEAD + 1,))],
    )
    return pl.pallas_call(
        functools.partial(_expert_body, n_steps=n_steps),
        grid_spec=grid_spec,
        out_shape=jax.ShapeDtypeStruct((n_steps * tile_rows, LANES), F32),
        compiler_params=pltpu.CompilerParams(dimension_semantics=("arbitrary",),
                                             vmem_limit_bytes=VMEM_LIMIT),
        name="moe_experts",
    )(tile_expert, tile_valid, chunk_src, xsl, wg, wu, wd)


def _combine_body(pos_ref, nxt_ref, x_ref, route_ref, ys_ref, out_ref, buf_ref, sem, *, tc, n_steps):
    i = pl.program_id(0)
    slot = lax.rem(i, 2)

    def fetch(rows_of, into):
        def start(r, _):
            for kk in range(2):
                _row_copy(_token_tile(ys_ref, rows_of[0, kk, r]), _token_tile(buf_ref.at[into, kk], r),
                          sem.at[into]).start(priority=kk)
            return 0
        lax.fori_loop(0, tc, start, 0, unroll=ROW_UNROLL)

    @pl.when(i == 0)
    def _():
        fetch(pos_ref, 0)

    @pl.when(i + 1 < n_steps)
    def _():
        fetch(nxt_ref, 1 - slot)

    for kk in range(2):
        _row_copy(ys_ref.at[pl.ds(0, tc * ROW_BLOCKS)], buf_ref.at[slot, kk], sem.at[slot]).wait()
    route = route_ref[...]
    out_ref[...] = (x_ref[...] + route[:, 0:1] * _from_token_tiles(buf_ref.at[slot, 0], tc)
                    + route[:, 1:2] * _from_token_tiles(buf_ref.at[slot, 1], tc))


def _combine(pos, x, route, ys, tc):
    t = x.shape[0]
    n_steps = t // tc
    return pl.pallas_call(
        functools.partial(_combine_body, tc=tc, n_steps=n_steps),
        grid=(n_steps,),
        in_specs=[pl.BlockSpec((1, 2, tc), lambda i: (i, 0, 0), memory_space=pltpu.SMEM),
                  pl.BlockSpec((1, 2, tc), lambda i: (i, 0, 0), memory_space=pltpu.SMEM),
                  pl.BlockSpec((tc, D_MODEL), lambda i: (i, 0)),
                  pl.BlockSpec((tc, LANES), lambda i: (i, 0)),
                  pl.BlockSpec(memory_space=pl.ANY)],
        out_specs=pl.BlockSpec((tc, D_MODEL), lambda i: (i, 0)),
        out_shape=jax.ShapeDtypeStruct((t, D_MODEL), F32),
        scratch_shapes=[pltpu.VMEM((2, 2, tc * ROW_BLOCKS, LANES), F32), pltpu.SemaphoreType.DMA((2,))],
        compiler_params=pltpu.CompilerParams(dimension_semantics=("arbitrary",)),
        name="moe_combine",
    )(pos, jnp.concatenate([pos[1:], pos[-1:]], axis=0), x, route, ys)


def _expert_tiles(t, tm):
    return (2 * t + (t // tm) * N_EXPERTS * (CHUNK_ROWS - 1)) // ROW_TILE + N_EXPERTS


def _routing_plan(route_t, counts, tm):
    i32 = jnp.int32
    t = route_t.shape[1]
    n_tok_tiles = t // tm
    local_chunks = _local_rows(tm) // CHUNK_ROWS
    n_tiles = _expert_tiles(t, tm)
    cnt = counts[:, 0, N_GROUPS:N_GROUPS + N_EXPERTS].astype(i32)
    chunks = (cnt + CHUNK_ROWS - 1) // CHUNK_ROWS
    first_local = jnp.cumsum(chunks, axis=1) - chunks
    upto = jnp.cumsum(chunks, axis=0)
    before = upto - chunks
    n_chunks = upto[-1]
    n_exp_tiles = (n_chunks + CHUNKS_PER_TILE - 1) // CHUNKS_PER_TILE
    tile_end = jnp.cumsum(n_exp_tiles)
    tile_start = tile_end - n_exp_tiles
    used = tile_end[-1]
    tiles = jnp.arange(n_tiles, dtype=i32)
    valid = tiles < used
    last = jnp.maximum(used - 1, 0)
    expert_of = jnp.sum((tile_end[None, :] <= tiles[:, None]).astype(i32), axis=1)
    tile_expert = jnp.where(valid, expert_of, jnp.sum(jnp.where(tiles == last, expert_of, 0)))

    sel = tile_expert[:, None] == jnp.arange(N_EXPERTS, dtype=i32)[None, :]
    per_tile = lambda table: jnp.sum(jnp.where(sel[:, :, None], table.T[None], 0), axis=1)
    per_tile1 = lambda vec: jnp.sum(jnp.where(sel, vec[None, :], 0), axis=1)
    upto_g, before_g, first_g = per_tile(upto), per_tile(before), per_tile(first_local)
    k = ((tiles - per_tile1(tile_start))[:, None] * CHUNKS_PER_TILE
         + jnp.arange(CHUNKS_PER_TILE, dtype=i32)[None, :])
    k_valid = valid[:, None] & (k < per_tile1(n_chunks)[:, None])
    src_tile = jnp.minimum(jnp.sum((upto_g[:, None, :] <= k[:, :, None]).astype(i32), axis=2), n_tok_tiles - 1)
    at = src_tile[:, :, None] == jnp.arange(n_tok_tiles, dtype=i32)[None, None, :]
    before_at = jnp.sum(jnp.where(at, before_g[:, None, :], 0), axis=2)
    first_at = jnp.sum(jnp.where(at, first_g[:, None, :], 0), axis=2)
    chunk = src_tile * local_chunks + first_at + (k - before_at)
    chunk_src = jnp.where(k_valid, chunk, local_chunks - 1).reshape(-1).astype(i32)

    base = tile_start[None, :] * ROW_TILE + CHUNK_ROWS * (before - first_local)
    base_t = jnp.repeat(base.T, tm, axis=1)
    experts = jnp.arange(N_EXPERTS, dtype=i32)[:, None]

    def position(e_row, local_row):
        e = e_row.astype(i32)[None, :]
        return jnp.sum(jnp.where(e == experts, base_t, 0), axis=0) + local_row.astype(i32)

    pos = jnp.stack([position(route_t[2], route_t[4]).reshape(n_tok_tiles, tm),
                     position(route_t[3], route_t[5]).reshape(n_tok_tiles, tm)], axis=1)
    return pos, tile_expert, valid.astype(i32), chunk_src


def _rope_tables(positions):
    half = ROPE // 2
    inv_freq = ROPE_THETA ** (-jnp.arange(half, dtype=F32) / half)
    ang = inv_freq[:, None] * positions.astype(F32).reshape(1, -1)
    spread = lambda a: jnp.tile(a.T, (1, LANES // half))
    return spread(jnp.cos(ang)), spread(jnp.sin(ang))


def _head_slots(w, width):
    k = w.shape[0]
    w = w.reshape(k, HEADS, width)
    return jnp.pad(w, ((0, 0), (0, 0), (0, LANES - width))).reshape(k, HEADS * LANES)


def _layer_weights(wi, w_uq, w_ukv, q_head_g, k_head_g, w_router_group, w_router_expert,
                   w_attn_proj, w_glu, w_out):
    kr = jnp.pad(wi[:, C_KR:C_KR + ROPE], ((0, 0), (NOPE, LANES - QK_DIM)))
    win = jnp.concatenate([wi[:, :C_KR], kr, wi[:, C_KR + ROPE:]], axis=1).astype(BF16)
    wuq = _head_slots(w_uq, QK_DIM).astype(BF16)
    kv = w_ukv.reshape(KV_LORA, HEADS, NOPE + V_DIM)
    wuk = _head_slots(kv[:, :, :NOPE].reshape(KV_LORA, HEADS * NOPE), NOPE)
    wuv = _head_slots(kv[:, :, NOPE:].reshape(KV_LORA, HEADS * V_DIM), V_DIM)
    wukv = jnp.concatenate([wuk, wuv], axis=1).astype(BF16)
    pad_g = lambda g: jnp.tile(jnp.pad(g.astype(F32), (0, LANES - QK_DIM)), HEADS)[None, :]
    gq = pad_g(q_head_g) * (QK_DIM ** -0.5 * math.log2(math.e))
    gk = pad_g(k_head_g)
    wr = jnp.concatenate([w_router_group, w_router_expert], axis=1).astype(F32)
    wr = jnp.pad(wr, ((0, 0), (0, LANES - wr.shape[1])))
    wrh = wr.astype(BF16)
    wrl = (wr - wrh.astype(F32)).astype(BF16)
    return (win, wuq, wukv, gq, gk, wrh, wrl,
            w_attn_proj.astype(BF16), w_glu.astype(BF16), w_out.astype(BF16))


def kernel(x, positions, norm_mix_g, w_in, q_lora_g, w_uq, kv_lora_g, w_ukv, q_head_g, k_head_g, w_attn_proj, lam_re, lam_im, log_dt, b_re, b_im, c_re, c_im, d_skip, w_glu, w_out, norm_ffn_g, w_router_group, w_router_expert, w_exp_gate, w_exp_up, w_exp_down):
    bsz, seq, _ = x.shape
    t = bsz * seq
    depth = w_in.shape[0]
    tm = 512 if t % 512 == 0 else 256
    tq = 256
    nsub = 2 if seq % (2 * tq) == 0 else 1
    bb = 4 if bsz % 4 == 0 else (2 if bsz % 2 == 0 else 1)
    row1 = lambda g: g.astype(F32)[None, :]

    cos_t, sin_t = _rope_tables(positions)
    xf = x.reshape(t, D_MODEL).astype(F32)
    for l in range(depth):
        win, wuq, wukv, gq, gk, wrh, wrl, wap, wglu, wout = _layer_weights(
            w_in[l], w_uq[l], w_ukv[l], q_head_g[l], k_head_g[l], w_router_group[l], w_router_expert[l],
            w_attn_proj[l], w_glu[l], w_out[l])
        tables = _ssm_tables(lam_re[l], lam_im[l], log_dt[l], b_re[l], b_im[l], c_re[l], c_im[l], d_skip[l])
        q, k, v, u, gates = _k1(xf, row1(norm_mix_g[l]), win, row1(q_lora_g[l]), wuq,
                                row1(kv_lora_g[l]), wukv, gq, gk, cos_t, sin_t, tm)
        o = _attention(q.reshape(bsz, seq, -1), k.reshape(bsz, seq, -1), v.reshape(bsz, seq, -1), tq, nsub)
        ys = _ssm(u.reshape(bsz, seq, SSM_WIDTH), *tables, bb)
        xn, xsl, route, route_t, counts = _k4(xf, o.reshape(t, -1), ys.reshape(t, SSM_WIDTH), gates,
                                              wap, wglu, wout, row1(norm_ffn_g[l]), wrh, wrl, tm)
        pos, tile_expert, tile_valid, chunk_src = _routing_plan(route_t, counts, tm)
        ye = _experts(tile_expert, tile_valid, chunk_src, xsl, w_exp_gate, w_exp_up, w_exp_down, l)
        xf = _combine(pos, xn, route, ye, tm)
    return xf.reshape(bsz, seq, D_MODEL).astype(x.dtype)
```
